```python
import jax, jax.numpy as jnp
from jax import lax
import numpy as np

D_MODEL = 1024
BATCH = 8
SEQ = 8192
DEPTH = 1
DEC_BATCH = 32
DEC_SEQ = 32
PAST_LEN = 2048

CHUNK = 64
Q_BLOCK = 128
N_HEADS = 8
QK_NOPE = 64
QK_ROPE = 32
V_HEAD = 64
KV_LORA = 256
Q_LORA = 768
D_ATTN = N_HEADS * V_HEAD
D_CONV = 512
CONV_W = 3
D_MIX = D_ATTN + D_CONV
D_IN_PROJ = Q_LORA + KV_LORA + QK_ROPE + 3 * D_CONV
D_FF = ((8 * D_MODEL + 3 * 256 - 1) // (3 * 256)) * 256
ROPE_THETA = 10000.0
EPS = 1e-6
SM_SCALE = (QK_NOPE + QK_ROPE) ** -0.5

kernel_name = "hybrid_mla_shortconv_streaming_step"


def rms_norm(x, g):
    xf = x.astype(jnp.float32)
    y = xf * lax.rsqrt(jnp.mean(jnp.square(xf), axis=-1, keepdims=True) + EPS)
    return (y * g.astype(jnp.float32)).astype(x.dtype)


def apply_rope(x, pos):
    half = QK_ROPE // 2
    inv_freq = ROPE_THETA ** (-jnp.arange(half, dtype=jnp.float32) / half)
    ang = pos.astype(jnp.float32)[:, None] * inv_freq[None, :]
    cos = jnp.cos(ang)[None, :, None, :]
    sin = jnp.sin(ang)[None, :, None, :]
    xf = x.astype(jnp.float32)
    x1, x2 = xf[..., :half], xf[..., half:]
    return jnp.concatenate([x1 * cos - x2 * sin, x1 * sin + x2 * cos], axis=-1).astype(x.dtype)


def attend(q_nope, q_rope, q_pos, k_nope, k_rope, v, k_pos):
    s = (jnp.einsum("bqhd,bkhd->bhqk", q_nope, k_nope)
         + jnp.einsum("bqhr,bkr->bhqk", q_rope, k_rope)).astype(jnp.float32) * SM_SCALE
    allowed = (k_pos[None, :] // CHUNK) <= (q_pos[:, None] // CHUNK)
    s = jnp.where(allowed[None, None], s, jnp.finfo(jnp.float32).min)
    p = jax.nn.softmax(s, axis=-1).astype(v.dtype)
    return jnp.einsum("bhqk,bkhd->bqhd", p, v)


def chunk_causal_attention(q_nope, q_rope, q_pos, k_nope, k_rope, v, k_pos):
    B, S = q_nope.shape[0], q_nope.shape[1]
    if S <= Q_BLOCK:
        return attend(q_nope, q_rope, q_pos, k_nope, k_rope, v, k_pos)
    nb = S // Q_BLOCK

    def to_blocks(t):
        return jnp.moveaxis(t.reshape(B, nb, Q_BLOCK, *t.shape[2:]), 1, 0)

    qp = q_pos.reshape(nb, Q_BLOCK)
    out = lax.map(lambda a: attend(a[0], a[1], a[2], k_nope, k_rope, v, k_pos),
                  (to_blocks(q_nope), to_blocks(q_rope), qp))
    return jnp.moveaxis(out, 0, 1).reshape(B, S, N_HEADS, V_HEAD)


def hybrid_mixer(h, pos, past_latent, past_k_rope, conv_prev,
                 w_in, q_norm_g, w_uq, kv_norm_g, w_ukv, conv_w, conv_b, w_out):
    B, S, _ = h.shape
    proj = h @ w_in
    i1 = Q_LORA
    i2 = i1 + KV_LORA
    i3 = i2 + QK_ROPE
    i4 = i3 + D_CONV
    i5 = i4 + D_CONV
    c_q, c_kv, k_r, u, g_b, g_c = jnp.split(proj, [i1, i2, i3, i4, i5], axis=-1)

    q = (rms_norm(c_q, q_norm_g) @ w_uq).reshape(B, S, N_HEADS, QK_NOPE + QK_ROPE)
    q_nope = q[..., :QK_NOPE]
    q_rope = apply_rope(q[..., QK_NOPE:], pos)
    latent = rms_norm(c_kv, kv_norm_g)
    k_rope_new = apply_rope(k_r[:, :, None, :], pos)[:, :, 0, :]
    if past_latent is None:
        all_latent, all_k_rope, k_pos = latent, k_rope_new, pos
    else:
        P = past_latent.shape[1]
        all_latent = jnp.concatenate([past_latent, latent], axis=1)
        all_k_rope = jnp.concatenate([past_k_rope, k_rope_new], axis=1)
        k_pos = jnp.concatenate([jnp.arange(P, dtype=jnp.int32), pos])
    T = all_latent.shape[1]
    kv = (all_latent @ w_ukv).reshape(B, T, N_HEADS, QK_NOPE + V_HEAD)
    k_nope, v = kv[..., :QK_NOPE], kv[..., QK_NOPE:]
    attn = chunk_causal_attention(q_nope, q_rope, pos, k_nope, all_k_rope, v, k_pos)

    gated_in = g_c * u
    ext = jnp.concatenate([conv_prev, gated_in], axis=1)
    conv = (conv_w[0] * ext[:, :S] + conv_w[1] * ext[:, 1:S + 1]
            + conv_w[2] * ext[:, 2:S + 2] + conv_b)
    y_conv = g_b * conv

    mix = jnp.concatenate([attn.reshape(B, S, D_ATTN), y_conv], axis=-1) @ w_out
    return mix, latent, k_rope_new, ext[:, S:]


def layer(x, c, pos, past_latent, past_k_rope, conv_prev,
          w_ada, b_ada, norm_mix_g, w_in, q_norm_g, w_uq, kv_norm_g, w_ukv,
          conv_w, conv_b, w_out, norm_ffn_g, w_gate, w_up, w_down):
    mod = jax.nn.silu(c) @ w_ada + b_ada
    sh1, sc1, g1, sh2, sc2, g2 = jnp.split(mod[:, None, :], 6, axis=-1)
    h = rms_norm(x, norm_mix_g) * (1 + sc1) + sh1
    mix, latent, k_rope_new, conv_state = hybrid_mixer(
        h, pos, past_latent, past_k_rope, conv_prev,
        w_in, q_norm_g, w_uq, kv_norm_g, w_ukv, conv_w, conv_b, w_out)
    x = x + g1 * mix
    h = rms_norm(x, norm_ffn_g) * (1 + sc2) + sh2
    ff = (jax.nn.silu(h @ w_gate) * (h @ w_up)) @ w_down
    x = x + g2 * ff
    return x, latent, k_rope_new, conv_state


def setup_inputs(seed: int = 0) -> dict:
    key = jax.random.key(seed)
    ks = jax.random.split(key, 24)

    def nrm(k, shape, s):
        return jax.random.normal(k, shape, jnp.float32) * s

    return {
        "x_prompt": nrm(ks[0], (BATCH, SEQ, D_MODEL), 1.0),
        "x_sample": nrm(ks[1], (DEC_BATCH, DEC_SEQ, D_MODEL), 1.0),
        "c_prompt": nrm(ks[2], (BATCH, D_MODEL), 1.0),
        "c_sample": nrm(ks[3], (DEC_BATCH, D_MODEL), 1.0),
        "cache_kv_latent": nrm(ks[4], (DEPTH, DEC_BATCH, PAST_LEN, KV_LORA), 1.0),
        "cache_k_rope": nrm(ks[5], (DEPTH, DEC_BATCH, PAST_LEN, QK_ROPE), 1.0),
        "state_conv": nrm(ks[6], (DEPTH, DEC_BATCH, CONV_W - 1, D_CONV), 0.5),
        "w_ada": nrm(ks[7], (DEPTH, D_MODEL, 6 * D_MODEL), 0.5 * D_MODEL ** -0.5),
        "b_ada": nrm(ks[8], (DEPTH, 6 * D_MODEL), 0.02),
        "norm_mix_g": 1.0 + nrm(ks[9], (DEPTH, D_MODEL), 0.02),
        "w_in": nrm(ks[10], (DEPTH, D_MODEL, D_IN_PROJ), D_MODEL ** -0.5),
        "q_norm_g": 1.0 + nrm(ks[11], (DEPTH, Q_LORA), 0.02),
        "w_uq": nrm(ks[12], (DEPTH, Q_LORA, N_HEADS * (QK_NOPE + QK_ROPE)), Q_LORA ** -0.5),
        "kv_norm_g": 1.0 + nrm(ks[13], (DEPTH, KV_LORA), 0.02),
        "w_ukv": nrm(ks[14], (DEPTH, KV_LORA, N_HEADS * (QK_NOPE + V_HEAD)), KV_LORA ** -0.5),
        "conv_w": nrm(ks[15], (DEPTH, CONV_W, D_CONV), CONV_W ** -0.5),
        "conv_b": nrm(ks[16], (DEPTH, D_CONV), 0.02),
        "w_out": nrm(ks[17], (DEPTH, D_MIX, D_MODEL), D_MIX ** -0.5),
        "norm_ffn_g": 1.0 + nrm(ks[18], (DEPTH, D_MODEL), 0.02),
        "w_gate": nrm(ks[19], (DEPTH, D_MODEL, D_FF), D_MODEL ** -0.5),
        "w_up": nrm(ks[20], (DEPTH, D_MODEL, D_FF), D_MODEL ** -0.5),
        "w_down": nrm(ks[21], (DEPTH, D_FF, D_MODEL), D_FF ** -0.5),
        "final_norm_g": 1.0 + nrm(ks[22], (D_MODEL,), 0.02),
    }


def reference(x_prompt, x_sample, c_prompt, c_sample, cache_kv_latent, cache_k_rope, state_conv,
              w_ada, b_ada, norm_mix_g, w_in, q_norm_g, w_uq, kv_norm_g, w_ukv,
              conv_w, conv_b, w_out, norm_ffn_g, w_gate, w_up, w_down, final_norm_g):
    S = x_prompt.shape[1]
    P = cache_kv_latent.shape[2]
    Sd = x_sample.shape[1]
    pos_p = jnp.arange(S, dtype=jnp.int32)
    pos_s = P + jnp.arange(Sd, dtype=jnp.int32)
    conv_zero = jnp.zeros((x_prompt.shape[0], CONV_W - 1, D_CONV), x_prompt.dtype)

    xp, xs = x_prompt, x_sample
    lat_p, kr_p, cv_p, lat_s, kr_s, cv_s = [], [], [], [], [], []
    for l in range(DEPTH):
        w = (w_ada[l], b_ada[l], norm_mix_g[l], w_in[l], q_norm_g[l], w_uq[l], kv_norm_g[l], w_ukv[l],
             conv_w[l], conv_b[l], w_out[l], norm_ffn_g[l], w_gate[l], w_up[l], w_down[l])
        xp, a, b, c = layer(xp, c_prompt, pos_p, None, None, conv_zero, *w)
        lat_p.append(a)
        kr_p.append(b)
        cv_p.append(c)
        xs, a, b, c = layer(xs, c_sample, pos_s, cache_kv_latent[l], cache_k_rope[l], state_conv[l], *w)
        lat_s.append(a)
        kr_s.append(b)
        cv_s.append(c)

    y_prompt = rms_norm(xp, final_norm_g)
    y_sample = rms_norm(xs, final_norm_g)
    new_latent_p = jnp.stack(lat_p)
    new_k_rope_p = jnp.stack(kr_p)
    new_conv_p = jnp.stack(cv_p)
    new_latent_s = jnp.stack(lat_s)
    new_k_rope_s = jnp.stack(kr_s)
    new_conv_s = jnp.stack(cv_s)
    return (y_prompt, y_sample, new_latent_p, new_k_rope_p, new_conv_p, new_latent_s, new_k_rope_s, new_conv_s)
```

```python
import functools

import jax
import jax.numpy as jnp
from jax import lax
from jax.experimental import pallas as pl
from jax.experimental.pallas import tpu as pltpu

CHUNK = 64
N_HEADS = 8
QK_NOPE = 64
QK_ROPE = 32
V_HEAD = 64
KV_LORA = 256
Q_LORA = 768
D_CONV = 512
CONV_W = 3
ROPE_THETA = 10000.0
EPS = 1e-6
SM_SCALE = (QK_NOPE + QK_ROPE) ** -0.5

HALF = QK_ROPE // 2
HEAD_SLAB = 128
D_SLAB = N_HEADS * HEAD_SLAB
D_ATTN = N_HEADS * V_HEAD
V_ROWS = V_HEAD + 16

C_Q = 0
C_KV = C_Q + Q_LORA
C_KR = C_KV + KV_LORA
C_KRS = C_KR + HEAD_SLAB
C_U = C_KRS + HEAD_SLAB
C_GB = C_U + D_CONV
C_GC = C_GB + D_CONV
D_PROJ = C_GC + D_CONV

VMEM_LIMIT_BYTES = 56 * 1024 * 1024

BF16 = jnp.bfloat16
F32 = jnp.float32
NEG = float(jnp.finfo(jnp.float32).min)


def _rms(x, g):
    ms = jnp.mean(x * x, axis=-1, keepdims=True)
    return x * lax.rsqrt(ms + EPS) * g


def _dot(a, b):
    return jnp.dot(a, b, preferred_element_type=F32)


def _dot_nt(a, b):
    return lax.dot_general(a, b, (((1,), (1,)), ((), ())), preferred_element_type=F32)


def _const_spec(shape):
    nd = len(shape)
    return pl.BlockSpec(shape, lambda *_: (0,) * nd, pipeline_mode=pl.Buffered(1))


def _mod_kernel(c_ref, w_ref, b_ref, o_ref):
    c = c_ref[...]
    a = (c * jax.nn.sigmoid(c)).astype(BF16)
    o_ref[...] = _dot(a, w_ref[...]) + b_ref[...]


def _mod_call(c_all, w_ada, b_ada):
    n, d = c_all.shape
    dn = w_ada.shape[1]
    tn = dn // 4
    return pl.pallas_call(
        _mod_kernel,
        grid=(dn // tn,),
        in_specs=[pl.BlockSpec((n, d), lambda j: (0, 0)),
                  pl.BlockSpec((d, tn), lambda j: (0, j)),
                  pl.BlockSpec((1, tn), lambda j: (0, j))],
        out_specs=pl.BlockSpec((n, tn), lambda j: (0, j)),
        out_shape=jax.ShapeDtypeStruct((n, dn), F32),
        compiler_params=pltpu.CompilerParams(dimension_semantics=("arbitrary",),
                                             vmem_limit_bytes=VMEM_LIMIT_BYTES),
        name="mod",
    )(c_all, w_ada.astype(BF16), b_ada.reshape(1, dn))


def _pre_common(x_ref, mod_ref, cprev_ref, gmix_ref, w_in_ref, kvg_ref, convw_ref, convb_ref,
                cos128_ref, sin128_ref, yconv_ref, lat_ref, krope_ref, cstate_ref, carry_ref):
    nb, ts, d = x_ref.shape
    rows = nb * ts
    x = x_ref[...]
    mod = mod_ref[...]
    h = _rms(x, gmix_ref[...]) * (1.0 + mod[:, 1:2, :]) + mod[:, 0:1, :]
    proj = _dot(h.reshape(rows, d).astype(BF16), w_in_ref[...])

    latent = _rms(proj[:, C_KV:C_KV + KV_LORA], kvg_ref[...])
    lat_ref[...] = latent.reshape(nb, ts, KV_LORA)

    g1 = proj[:, C_KR:C_KR + HEAD_SLAB].reshape(nb, ts, HEAD_SLAB)
    g2 = proj[:, C_KRS:C_KRS + HEAD_SLAB].reshape(nb, ts, HEAD_SLAB)
    kr128 = g1 * cos128_ref[...][None] + g2 * sin128_ref[...][None]
    krope_ref[...] = kr128[:, :, :QK_ROPE]

    u = proj[:, C_U:C_U + D_CONV]
    g_b = proj[:, C_GB:C_GB + D_CONV].reshape(nb, ts, D_CONV)
    g_c = proj[:, C_GC:C_GC + D_CONV]
    gated = (g_c * u).reshape(nb, ts, D_CONV)

    @pl.when(pl.program_id(1) == 0)
    def _():
        carry_ref[:, 6:8, :] = cprev_ref[...]

    carry_ref[:, 8:8 + ts, :] = gated
    prev2 = carry_ref[:, 6:6 + ts, :]
    prev1 = carry_ref[:, 7:7 + ts, :]
    cw = convw_ref[...]
    conv = cw[0:1][None] * prev2 + cw[1:2][None] * prev1 + cw[2:3][None] * gated + convb_ref[...][None]
    yconv_ref[...] = (g_b * conv).astype(BF16)
    state = gated[:, ts - 2:ts, :]
    cstate_ref[...] = state
    carry_ref[:, 6:8, :] = state
    return proj, latent.astype(BF16), kr128


def _pre_prompt_kernel(x_ref, mod_ref, cprev_ref, gmix_ref, w_in_ref, kvg_ref, convw_ref, convb_ref,
                       cos128_ref, sin128_ref, qg_ref, w_uqT_ref, w_uk_ref, w_uvT_ref, cosT_ref, sinT_ref,
                       yconv_ref, lat_ref, krope_ref, cstate_ref, qT_ref, k_ref, vT_ref,
                       carry_ref, *, tk):
    nb, ts, _ = x_ref.shape
    proj, lat_bf, kr128 = _pre_common(
        x_ref, mod_ref, cprev_ref, gmix_ref, w_in_ref, kvg_ref, convw_ref, convb_ref,
        cos128_ref, sin128_ref, yconv_ref, lat_ref, krope_ref, cstate_ref, carry_ref)
    kr128 = kr128.reshape(ts, HEAD_SLAB)

    cqn = _rms(proj[:, C_Q:C_Q + Q_LORA], qg_ref[...]).astype(BF16)
    qT = _dot_nt(w_uqT_ref[...], cqn) * SM_SCALE
    c = cosT_ref[...]
    s = sinT_ref[...]
    for h in range(N_HEADS):
        b0 = h * HEAD_SLAB
        x1 = qT[b0:b0 + HALF]
        x2 = qT[b0 + HALF:b0 + QK_ROPE]
        qT_ref[0, b0:b0 + HALF, :] = (x1 * c - x2 * s).astype(BF16)
        qT_ref[0, b0 + HALF:b0 + QK_ROPE, :] = (x1 * s + x2 * c).astype(BF16)
        qT_ref[0, b0 + QK_ROPE:b0 + HEAD_SLAB, :] = qT[b0 + QK_ROPE:b0 + HEAD_SLAB].astype(BF16)

    k_nope = _dot(lat_bf, w_uk_ref[...])
    for h in range(N_HEADS):
        b0 = h * HEAD_SLAB
        k_ref[0, :, b0:b0 + HEAD_SLAB] = (k_nope[:, b0:b0 + HEAD_SLAB] + kr128).astype(BF16)

    vT = _dot_nt(w_uvT_ref[...], lat_bf)
    row = lax.broadcasted_iota(jnp.int32, (V_ROWS - V_HEAD, tk), 0)
    ones_rows = jnp.where(row == 0, 1.0, 0.0).astype(BF16)
    for h in range(N_HEADS):
        for cb in range(ts // tk):
            vT_ref[0, h, cb, 0:V_HEAD, :] = vT[h * V_HEAD:(h + 1) * V_HEAD, cb * tk:(cb + 1) * tk].astype(BF16)
            vT_ref[0, h, cb, V_HEAD:V_ROWS, :] = ones_rows


def _pre_sample_kernel(x_ref, mod_ref, cprev_ref, gmix_ref, w_in_ref, kvg_ref, convw_ref, convb_ref,
                       cos128_ref, sin128_ref, qg_ref, w_uq_ref, w_uqs_ref, cosf_ref, sinf_ref,
                       yconv_ref, lat_ref, krope_ref, cstate_ref, q_ref, carry_ref):
    nb, ts, _ = x_ref.shape
    proj, _, _ = _pre_common(
        x_ref, mod_ref, cprev_ref, gmix_ref, w_in_ref, kvg_ref, convw_ref, convb_ref,
        cos128_ref, sin128_ref, yconv_ref, lat_ref, krope_ref, cstate_ref, carry_ref)
    cqn = _rms(proj[:, C_Q:C_Q + Q_LORA], qg_ref[...]).astype(BF16)
    q = (_dot(cqn, w_uq_ref[...]) * SM_SCALE).reshape(nb, ts, D_SLAB)
    qs = (_dot(cqn, w_uqs_ref[...]) * SM_SCALE).reshape(nb, ts, D_SLAB)
    q_ref[...] = (q * cosf_ref[...][None] + qs * sinf_ref[...][None]).astype(BF16)


def _pre_call(x, mod, cprev, w, tabs, *, nb, ts, tk, prompt):
    b, s, d = x.shape
    grid = (b // nb, s // ts)
    tok = lambda width: pl.BlockSpec((nb, ts, width), lambda i, j: (i, j, 0))
    per_b = lambda rows, width: pl.BlockSpec((nb, rows, width), lambda i, j: (i, 0, 0))
    in_specs = [tok(d), per_b(6, d), per_b(CONV_W - 1, D_CONV),
                _const_spec((1, d)), _const_spec((d, D_PROJ)), _const_spec((1, KV_LORA)),
                _const_spec((CONV_W, D_CONV)), _const_spec((1, D_CONV)),
                pl.BlockSpec((ts, HEAD_SLAB), lambda i, j: (j, 0)),
                pl.BlockSpec((ts, HEAD_SLAB), lambda i, j: (j, 0)),
                _const_spec((1, Q_LORA))]
    args = [x, mod, cprev, w["g_mix"], w["w_in"], w["kv_g"], w["conv_w"], w["conv_b"],
            tabs["cos128"], tabs["sin128"], w["q_g"]]
    out_specs = [tok(D_CONV), tok(KV_LORA), tok(QK_ROPE), per_b(CONV_W - 1, D_CONV)]
    out_shape = [jax.ShapeDtypeStruct((b, s, D_CONV), BF16),
                 jax.ShapeDtypeStruct((b, s, KV_LORA), F32),
                 jax.ShapeDtypeStruct((b, s, QK_ROPE), F32),
                 jax.ShapeDtypeStruct((b, CONV_W - 1, D_CONV), F32)]
    if prompt:
        assert nb == 1 and ts % tk == 0
        kernel = functools.partial(_pre_prompt_kernel, tk=tk)
        in_specs += [_const_spec((D_SLAB, Q_LORA)), _const_spec((KV_LORA, D_SLAB)),
                     _const_spec((D_ATTN, KV_LORA)),
                     pl.BlockSpec((HALF, ts), lambda i, j: (0, j)),
                     pl.BlockSpec((HALF, ts), lambda i, j: (0, j))]
        args += [w["w_uqT"], w["w_uk"], w["w_uvT"], tabs["cosT"], tabs["sinT"]]
        out_specs += [pl.BlockSpec((1, D_SLAB, ts), lambda i, j: (i, 0, j)),
                      tok(D_SLAB),
                      pl.BlockSpec((1, N_HEADS, ts // tk, V_ROWS, tk), lambda i, j: (i, 0, j, 0, 0))]
        out_shape += [jax.ShapeDtypeStruct((b, D_SLAB, s), BF16),
                      jax.ShapeDtypeStruct((b, s, D_SLAB), BF16),
                      jax.ShapeDtypeStruct((b, N_HEADS, s // tk, V_ROWS, tk), BF16)]
    else:
        assert ts == s
        kernel = _pre_sample_kernel
        in_specs += [_const_spec((Q_LORA, D_SLAB)), _const_spec((Q_LORA, D_SLAB)),
                     _const_spec((ts, D_SLAB)), _const_spec((ts, D_SLAB))]
        args += [w["w_uq"], w["w_uqs"], tabs["cosf"], tabs["sinf"]]
        out_specs += [tok(D_SLAB)]
        out_shape += [jax.ShapeDtypeStruct((b, s, D_SLAB), BF16)]
    return pl.pallas_call(
        kernel, grid=grid, in_specs=in_specs, out_specs=out_specs, out_shape=out_shape,
        scratch_shapes=[pltpu.VMEM((nb, ts + 8, D_CONV), F32)],
        compiler_params=pltpu.CompilerParams(dimension_semantics=("arbitrary", "arbitrary"),
                                             vmem_limit_bytes=VMEM_LIMIT_BYTES),
        name="pre_prompt" if prompt else "pre_sample",
    )(*args)


def _attn_kernel(qT_ref, k_ref, vT_ref, o_ref, acc_ref, m_ref, *, tq, tk, heads):
    i = pl.program_id(2)
    sub = tq // tk
    m_ref[...] = jnp.full(m_ref.shape, NEG, F32)
    acc_ref[...] = jnp.zeros(acc_ref.shape, F32)

    def step(j, col0, masked):
        k0 = pl.multiple_of(j * tk, tk)
        for hh in range(heads):
            kb = k_ref[0, pl.ds(k0, tk), hh * HEAD_SLAB:(hh + 1) * HEAD_SLAB]
            qT = qT_ref[0, hh * HEAD_SLAB:(hh + 1) * HEAD_SLAB, col0:tq]
            sT = _dot(kb, qT)
            if masked:
                kpos = k0 + lax.broadcasted_iota(jnp.int32, (tk, 1), 0)
                qpos = i * tq + col0 + lax.broadcasted_iota(jnp.int32, (1, tq - col0), 1)
                sT = jnp.where(kpos <= (qpos | (CHUNK - 1)), sT, NEG)
            m_old = m_ref[hh, :, col0:tq]
            m_new = jnp.maximum(m_old, jnp.max(sT, axis=0, keepdims=True))
            alpha = jnp.exp(m_old - m_new)
            pT = jnp.exp(sT - m_new).astype(BF16)
            pv = _dot(vT_ref[0, hh, j], pT)
            acc_ref[hh, :, col0:tq] = acc_ref[hh, :, col0:tq] * alpha + pv
            m_ref[hh, :, col0:tq] = m_new

    def body(j, carry):
        step(j, 0, False)
        return carry

    n_full = i * sub
    lax.fori_loop(0, n_full, body, 0)
    for dblk in range(sub):
        step(n_full + dblk, dblk * tk, True)

    outs = []
    for hh in range(heads):
        acc = acc_ref[hh]
        outs.append(acc[0:V_HEAD] / acc[V_HEAD:V_HEAD + 1])
    oT = jnp.concatenate(outs, axis=0)
    o_ref[0] = oT.T.astype(BF16)


def _attn_call(qT, k, vT, *, tq, tk, heads=2):
    b, _, s = qT.shape
    nkb = s // tk
    assert tk % CHUNK == 0 and tq % tk == 0 and s % tq == 0 and N_HEADS % heads == 0
    grid = (b, N_HEADS // heads, s // tq)
    return pl.pallas_call(
        functools.partial(_attn_kernel, tq=tq, tk=tk, heads=heads),
        grid=grid,
        in_specs=[pl.BlockSpec((1, heads * HEAD_SLAB, tq), lambda bi, h, i: (bi, h, i)),
                  pl.BlockSpec((1, s, heads * HEAD_SLAB), lambda bi, h, i: (bi, 0, h)),
                  pl.BlockSpec((1, heads, nkb, V_ROWS, tk), lambda bi, h, i: (bi, h, 0, 0, 0))],
        out_specs=pl.BlockSpec((1, tq, heads * V_HEAD), lambda bi, h, i: (bi, i, h)),
        out_shape=jax.ShapeDtypeStruct((b, s, D_ATTN), BF16),
        scratch_shapes=[pltpu.VMEM((heads, V_ROWS, tq), F32), pltpu.VMEM((heads, 1, tq), F32)],
        compiler_params=pltpu.CompilerParams(dimension_semantics=("arbitrary", "arbitrary", "arbitrary"),
                                             vmem_limit_bytes=VMEM_LIMIT_BYTES),
        name="attn_prompt",
    )(qT, k, vT)


def _sattn_kernel(q_ref, latp_ref, krp_ref, latn_ref, krn_ref, w_uk_ref, w_uv_ref, o_ref, o_scr):
    latp = latp_ref[0].astype(BF16)
    latn = latn_ref[0].astype(BF16)
    krp = krp_ref[0].astype(BF16)
    krn = krn_ref[0].astype(BF16)
    kp = _dot(latp, w_uk_ref[...]).astype(BF16)
    kn = _dot(latn, w_uk_ref[...]).astype(BF16)
    vp = _dot(latp, w_uv_ref[...]).astype(BF16)
    vn = _dot(latn, w_uv_ref[...]).astype(BF16)
    for h in range(N_HEADS):
        sl = slice(h * HEAD_SLAB, (h + 1) * HEAD_SLAB)
        qh = q_ref[0, :, sl]
        qr = qh[:, 0:QK_ROPE]
        s_p = _dot_nt(qh, kp[:, sl]) + _dot_nt(qr, krp)
        s_n = _dot_nt(qh, kn[:, sl]) + _dot_nt(qr, krn)
        m = jnp.maximum(jnp.max(s_p, axis=-1, keepdims=True), jnp.max(s_n, axis=-1, keepdims=True))
        p_p = jnp.exp(s_p - m)
        p_n = jnp.exp(s_n - m)
        l = jnp.sum(p_p, axis=-1, keepdims=True) + jnp.sum(p_n, axis=-1, keepdims=True)
        o = _dot((p_p / l).astype(BF16), vp[:, sl]) + _dot((p_n / l).astype(BF16), vn[:, sl])
        o_scr[:, h * V_HEAD:(h + 1) * V_HEAD] = o[:, 0:V_HEAD]
    o_ref[0] = o_scr[...].astype(BF16)


def _sattn_call(q, lat_past, kr_past, lat_new, kr_new, w_uk, w_uv):
    b, sd, _ = q.shape
    p = lat_past.shape[1]
    assert (p + sd - 1) // CHUNK <= p // CHUNK
    return pl.pallas_call(
        _sattn_kernel,
        grid=(b,),
        in_specs=[pl.BlockSpec((1, sd, D_SLAB), lambda i: (i, 0, 0)),
                  pl.BlockSpec((1, p, KV_LORA), lambda i: (i, 0, 0)),
                  pl.BlockSpec((1, p, QK_ROPE), lambda i: (i, 0, 0)),
                  pl.BlockSpec((1, sd, KV_LORA), lambda i: (i, 0, 0)),
                  pl.BlockSpec((1, sd, QK_ROPE), lambda i: (i, 0, 0)),
                  _const_spec((KV_LORA, D_SLAB)), _const_spec((KV_LORA, D_SLAB))],
        out_specs=pl.BlockSpec((1, sd, D_ATTN), lambda i: (i, 0, 0)),
        out_shape=jax.ShapeDtypeStruct((b, sd, D_ATTN), BF16),
        scratch_shapes=[pltpu.VMEM((sd, D_ATTN), F32)],
        compiler_params=pltpu.CompilerParams(dimension_semantics=("arbitrary",),
                                             vmem_limit_bytes=VMEM_LIMIT_BYTES),
        name="attn_sample",
    )(q, lat_past, kr_past, lat_new, kr_new, w_uk, w_uv)


def _post_kernel(x_ref, attn_ref, yconv_ref, mod_ref, w_oa_ref, w_oc_ref, gffn_ref,
                 w_gate_ref, w_up_ref, w_down_ref, gfin_ref, y_ref):
    nb, ts, d = x_ref.shape
    rows = nb * ts
    mod = mod_ref[...]
    a = attn_ref[...].reshape(rows, D_ATTN)
    yc = yconv_ref[...].reshape(rows, D_CONV)
    mix = _dot(a, w_oa_ref[...]) + _dot(yc, w_oc_ref[...])
    x1 = x_ref[...] + mod[:, 2:3, :] * mix.reshape(nb, ts, d)
    h = _rms(x1, gffn_ref[...]) * (1.0 + mod[:, 4:5, :]) + mod[:, 3:4, :]
    hb = h.reshape(rows, d).astype(BF16)
    gate = _dot(hb, w_gate_ref[...])
    up = _dot(hb, w_up_ref[...])
    act = ((gate * jax.nn.sigmoid(gate)) * up).astype(BF16)
    ff = _dot(act, w_down_ref[...])
    x2 = x1 + mod[:, 5:6, :] * ff.reshape(nb, ts, d)
    y_ref[...] = _rms(x2, gfin_ref[...])


def _post_call(x, attn, yconv, mod, w, *, nb, ts):
    b, s, d = x.shape
    dff = w["w_gate"].shape[1]
    tok = lambda width: pl.BlockSpec((nb, ts, width), lambda i, j: (i, j, 0))
    return pl.pallas_call(
        _post_kernel,
        grid=(b // nb, s // ts),
        in_specs=[tok(d), tok(D_ATTN), tok(D_CONV),
                  pl.BlockSpec((nb, 6, d), lambda i, j: (i, 0, 0)),
                  _const_spec((D_ATTN, d)), _const_spec((D_CONV, d)), _const_spec((1, d)),
                  _const_spec((d, dff)), _const_spec((d, dff)), _const_spec((dff, d)),
                  _const_spec((1, d))],
        out_specs=tok(d),
        out_shape=jax.ShapeDtypeStruct((b, s, d), F32),
        compiler_params=pltpu.CompilerParams(dimension_semantics=("arbitrary", "arbitrary"),
                                             vmem_limit_bytes=VMEM_LIMIT_BYTES),
        name="post",
    )(x, attn, yconv, mod, w["w_oa"], w["w_oc"], w["g_ffn"], w["w_gate"], w["w_up"], w["w_down"], w["g_fin"])


def _prep_weights(w_in, w_uq, w_ukv, w_out, norm_mix_g, q_norm_g, kv_norm_g, conv_w, conv_b,
                  norm_ffn_g, w_gate, w_up, w_down, final_norm_g):
    d = w_in.shape[0]
    i1 = Q_LORA
    i2 = i1 + KV_LORA
    i3 = i2 + QK_ROPE
    zpad = jnp.zeros((d, HEAD_SLAB - QK_ROPE), w_in.dtype)
    k_r = w_in[:, i2:i3]
    k_r_sw = jnp.concatenate([k_r[:, HALF:], k_r[:, :HALF]], axis=1)
    w_in_p = jnp.concatenate([w_in[:, :i2], k_r, zpad, k_r_sw, zpad, w_in[:, i3:]], axis=1)
    assert w_in_p.shape[1] == D_PROJ

    wq = w_uq.reshape(Q_LORA, N_HEADS, QK_NOPE + QK_ROPE)
    zq = jnp.zeros((Q_LORA, N_HEADS, HEAD_SLAB - QK_NOPE - QK_ROPE), w_uq.dtype)
    w_uq_p = jnp.concatenate([wq[..., QK_NOPE:], wq[..., :QK_NOPE], zq], axis=-1).reshape(Q_LORA, D_SLAB)
    zq2 = jnp.zeros((Q_LORA, N_HEADS, HEAD_SLAB - QK_ROPE), w_uq.dtype)
    w_uq_s = jnp.concatenate([wq[..., QK_NOPE + HALF:], wq[..., QK_NOPE:QK_NOPE + HALF], zq2],
                             axis=-1).reshape(Q_LORA, D_SLAB)

    wkv = w_ukv.reshape(KV_LORA, N_HEADS, QK_NOPE + V_HEAD)
    zk1 = jnp.zeros((KV_LORA, N_HEADS, QK_ROPE), w_ukv.dtype)
    zk2 = jnp.zeros((KV_LORA, N_HEADS, HEAD_SLAB - QK_ROPE - QK_NOPE), w_ukv.dtype)
    w_uk_p = jnp.concatenate([zk1, wkv[..., :QK_NOPE], zk2], axis=-1).reshape(KV_LORA, D_SLAB)
    w_uv = wkv[..., QK_NOPE:]
    zv = jnp.zeros((KV_LORA, N_HEADS, HEAD_SLAB - V_HEAD), w_ukv.dtype)
    w_uv_p = jnp.concatenate([w_uv, zv], axis=-1).reshape(KV_LORA, D_SLAB)
    w_uvT = w_uv.reshape(KV_LORA, D_ATTN).T

    return {
        "g_mix": norm_mix_g.reshape(1, -1), "w_in": w_in_p.astype(BF16),
        "q_g": q_norm_g.reshape(1, -1), "kv_g": kv_norm_g.reshape(1, -1),
        "conv_w": conv_w, "conv_b": conv_b.reshape(1, -1),
        "w_uq": w_uq_p.astype(BF16), "w_uqs": w_uq_s.astype(BF16), "w_uqT": w_uq_p.T.astype(BF16),
        "w_uk": w_uk_p.astype(BF16), "w_uv": w_uv_p.astype(BF16), "w_uvT": w_uvT.astype(BF16),
        "w_oa": w_out[:D_ATTN].astype(BF16), "w_oc": w_out[D_ATTN:].astype(BF16),
        "g_ffn": norm_ffn_g.reshape(1, -1),
        "w_gate": w_gate.astype(BF16), "w_up": w_up.astype(BF16), "w_down": w_down.astype(BF16),
        "g_fin": final_norm_g.reshape(1, -1),
    }


def _rope_tables(pos):
    n = pos.shape[0]
    inv_freq = ROPE_THETA ** (-jnp.arange(HALF, dtype=F32) / HALF)
    ang = pos.astype(F32)[:, None] * inv_freq[None, :]
    cos = jnp.cos(ang)
    sin = jnp.sin(ang)
    z96 = jnp.zeros((n, HEAD_SLAB - QK_ROPE), F32)
    cos128 = jnp.concatenate([cos, cos, z96], axis=1)
    sin128 = jnp.concatenate([-sin, sin, z96], axis=1)
    ones96 = jnp.ones((n, HEAD_SLAB - QK_ROPE), F32)
    cosf = jnp.tile(jnp.concatenate([cos, cos, ones96], axis=1), (1, N_HEADS))
    sinf = jnp.tile(sin128, (1, N_HEADS))
    return {"cos128": cos128, "sin128": sin128, "cosT": cos.T, "sinT": sin.T, "cosf": cosf, "sinf": sinf}


def kernel(x_prompt, x_sample, c_prompt, c_sample, cache_kv_latent, cache_k_rope, state_conv,
           w_ada, b_ada, norm_mix_g, w_in, q_norm_g, w_uq, kv_norm_g, w_ukv,
           conv_w, conv_b, w_out, norm_ffn_g, w_gate, w_up, w_down, final_norm_g):
    depth = w_in.shape[0]
    assert depth == 1
    bp, s, d = x_prompt.shape
    bs, sd, _ = x_sample.shape
    p = cache_kv_latent.shape[2]
    l = 0
    w = _prep_weights(w_in[l], w_uq[l], w_ukv[l], w_out[l], norm_mix_g[l], q_norm_g[l], kv_norm_g[l],
                      conv_w[l], conv_b[l], norm_ffn_g[l], w_gate[l], w_up[l], w_down[l], final_norm_g)
    tabs_p = _rope_tables(jnp.arange(s, dtype=jnp.int32))
    tabs_s = _rope_tables(p + jnp.arange(sd, dtype=jnp.int32))

    mod = _mod_call(jnp.concatenate([c_prompt, c_sample], axis=0), w_ada[l], b_ada[l])
    mod = mod.reshape(bp + bs, 6, d)
    mod_p, mod_s = mod[:bp], mod[bp:]

    tq, tk = 512, 256
    conv_zero = jnp.zeros((bp, CONV_W - 1, D_CONV), x_prompt.dtype)
    yconv_p, lat_p, kr_p, cv_p, qT, k_all, vT = _pre_call(
        x_prompt, mod_p, conv_zero, w, tabs_p, nb=1, ts=512, tk=tk, prompt=True)
    attn_p = _attn_call(qT, k_all, vT, tq=tq, tk=tk)
    y_prompt = _post_call(x_prompt, attn_p, yconv_p, mod_p, w, nb=1, ts=512)

    nb_s = 8
    yconv_s, lat_s, kr_s, cv_s, q_s = _pre_call(
        x_sample, mod_s, state_conv[l], w, tabs_s, nb=nb_s, ts=sd, tk=tk, prompt=False)
    attn_s = _sattn_call(q_s, cache_kv_latent[l], cache_k_rope[l], lat_s, kr_s, w["w_uk"], w["w_uv"])
    y_sample = _post_call(x_sample, attn_s, yconv_s, mod_s, w, nb=nb_s, ts=sd)

    return (y_prompt, y_sample, lat_p[None], kr_p[None], cv_p[None], lat_s[None], kr_s[None], cv_s[None])
```

```python
import functools

import jax
import jax.numpy as jnp
from jax import lax
from jax.experimental import pallas as pl
from jax.experimental.pallas import tpu as pltpu

CHUNK = 64
N_HEADS = 8
QK_NOPE = 64
QK_ROPE = 32
V_HEAD = 64
KV_LORA = 256
Q_LORA = 768
D_CONV = 512
CONV_W = 3
ROPE_THETA = 10000.0
EPS = 1e-6
SM_SCALE = (QK_NOPE + QK_ROPE) ** -0.5
LOG2E = 1.4426950408889634

HALF = QK_ROPE // 2
HEAD_SLAB = 128
D_SLAB = N_HEADS * HEAD_SLAB
D_ATTN = N_HEADS * V_HEAD
V_ROWS = V_HEAD + 16

C_Q = 0
C_KV = C_Q + Q_LORA
C_KR = C_KV + KV_LORA
C_KRS = C_KR + HEAD_SLAB
C_U = C_KRS + HEAD_SLAB
C_GB = C_U + D_CONV
C_GC = C_GB + D_CONV
D_PROJ = C_GC + D_CONV

VMEM_LIMIT_BYTES = 56 * 1024 * 1024

BF16 = jnp.bfloat16
F32 = jnp.float32
NEG = float(jnp.finfo(jnp.float32).min)


def _rms(x, g):
    ms = jnp.mean(x * x, axis=-1, keepdims=True)
    return x * lax.rsqrt(ms + EPS) * g


def _dot(a, b):
    return jnp.dot(a, b, preferred_element_type=F32)


def _dot_nt(a, b):
    return lax.dot_general(a, b, (((1,), (1,)), ((), ())), preferred_element_type=F32)


def _const_spec(shape):
    nd = len(shape)
    return pl.BlockSpec(shape, lambda *_: (0,) * nd, pipeline_mode=pl.Buffered(1))


def _mod_kernel(c_ref, w_ref, b_ref, o_ref):
    c = c_ref[...]
    a = (c * jax.nn.sigmoid(c)).astype(BF16)
    o_ref[...] = _dot(a, w_ref[...]) + b_ref[...]


def _mod_call(c_all, w_ada, b_ada):
    n, d = c_all.shape
    dn = w_ada.shape[1]
    tn = dn // 4
    return pl.pallas_call(
        _mod_kernel,
        grid=(dn // tn,),
        in_specs=[pl.BlockSpec((n, d), lambda j: (0, 0)),
                  pl.BlockSpec((d, tn), lambda j: (0, j)),
                  pl.BlockSpec((1, tn), lambda j: (0, j))],
        out_specs=pl.BlockSpec((n, tn), lambda j: (0, j)),
        out_shape=jax.ShapeDtypeStruct((n, dn), F32),
        compiler_params=pltpu.CompilerParams(dimension_semantics=("arbitrary",),
                                             vmem_limit_bytes=VMEM_LIMIT_BYTES),
        name="mod",
    )(c_all, w_ada.astype(BF16), b_ada.reshape(1, dn))


def _pre_common(x_ref, mod_ref, cprev_ref, gmix_ref, w_in_ref, kvg_ref, convw_ref, convb_ref,
                cos128_ref, sin128_ref, yconv_ref, lat_ref, krope_ref, cstate_ref, carry_ref):
    nb, ts, d = x_ref.shape
    rows = nb * ts
    x = x_ref[...]
    mod = mod_ref[...]
    h = _rms(x, gmix_ref[...]) * (1.0 + mod[:, 1:2, :]) + mod[:, 0:1, :]
    proj = _dot(h.reshape(rows, d).astype(BF16), w_in_ref[...])

    latent = _rms(proj[:, C_KV:C_KV + KV_LORA], kvg_ref[...])
    lat_ref[...] = latent.reshape(nb, ts, KV_LORA)

    g1 = proj[:, C_KR:C_KR + HEAD_SLAB].reshape(nb, ts, HEAD_SLAB)
    g2 = proj[:, C_KRS:C_KRS + HEAD_SLAB].reshape(nb, ts, HEAD_SLAB)
    kr128 = g1 * cos128_ref[...][None] + g2 * sin128_ref[...][None]
    krope_ref[...] = kr128[:, :, :QK_ROPE]

    u = proj[:, C_U:C_U + D_CONV]
    g_b = proj[:, C_GB:C_GB + D_CONV].reshape(nb, ts, D_CONV)
    g_c = proj[:, C_GC:C_GC + D_CONV]
    gated = (g_c * u).reshape(nb, ts, D_CONV)

    @pl.when(pl.program_id(1) == 0)
    def _():
        carry_ref[:, 6:8, :] = cprev_ref[...]

    carry_ref[:, 8:8 + ts, :] = gated
    prev2 = carry_ref[:, 6:6 + ts, :]
    prev1 = carry_ref[:, 7:7 + ts, :]
    cw = convw_ref[...]
    conv = cw[0:1][None] * prev2 + cw[1:2][None] * prev1 + cw[2:3][None] * gated + convb_ref[...][None]
    yconv_ref[...] = (g_b * conv).astype(BF16)
    state = gated[:, ts - 2:ts, :]
    cstate_ref[...] = state
    carry_ref[:, 6:8, :] = state
    return proj, latent.astype(BF16), kr128


def _pre_prompt_kernel(x_ref, mod_ref, cprev_ref, gmix_ref, w_in_ref, kvg_ref, convw_ref, convb_ref,
                       cos128_ref, sin128_ref, qg_ref, w_uqT_ref, w_uk_ref, w_uvT_ref, cosT_ref, sinT_ref,
                       yconv_ref, lat_ref, krope_ref, cstate_ref, qT_ref, k_ref, vT_ref,
                       carry_ref, *, tk):
    nb, ts, _ = x_ref.shape
    proj, lat_bf, kr128 = _pre_common(
        x_ref, mod_ref, cprev_ref, gmix_ref, w_in_ref, kvg_ref, convw_ref, convb_ref,
        cos128_ref, sin128_ref, yconv_ref, lat_ref, krope_ref, cstate_ref, carry_ref)
    kr128 = kr128.reshape(ts, HEAD_SLAB)

    cqn = _rms(proj[:, C_Q:C_Q + Q_LORA], qg_ref[...]).astype(BF16)
    qT = _dot_nt(w_uqT_ref[...], cqn) * (SM_SCALE * LOG2E)
    c = cosT_ref[...]
    s = sinT_ref[...]
    for h in range(N_HEADS):
        b0 = h * HEAD_SLAB
        x1 = qT[b0:b0 + HALF]
        x2 = qT[b0 + HALF:b0 + QK_ROPE]
        qT_ref[0, b0:b0 + HALF, :] = (x1 * c - x2 * s).astype(BF16)
        qT_ref[0, b0 + HALF:b0 + QK_ROPE, :] = (x1 * s + x2 * c).astype(BF16)
        qT_ref[0, b0 + QK_ROPE:b0 + HEAD_SLAB, :] = qT[b0 + QK_ROPE:b0 + HEAD_SLAB].astype(BF16)

    k_nope = _dot(lat_bf, w_uk_ref[...])
    for h in range(N_HEADS):
        b0 = h * HEAD_SLAB
        k_ref[0, :, b0:b0 + HEAD_SLAB] = (k_nope[:, b0:b0 + HEAD_SLAB] + kr128).astype(BF16)

    vT = _dot_nt(w_uvT_ref[...], lat_bf)
    row = lax.broadcasted_iota(jnp.int32, (V_ROWS - V_HEAD, tk), 0)
    ones_rows = jnp.where(row == 0, 1.0, 0.0).astype(BF16)
    for h in range(N_HEADS):
        for cb in range(ts // tk):
            vT_ref[0, h, cb, 0:V_HEAD, :] = vT[h * V_HEAD:(h + 1) * V_HEAD, cb * tk:(cb + 1) * tk].astype(BF16)
            vT_ref[0, h, cb, V_HEAD:V_ROWS, :] = ones_rows


def _pre_sample_kernel(x_ref, mod_ref, cprev_ref, gmix_ref, w_in_ref, kvg_ref, convw_ref, convb_ref,
                       cos128_ref, sin128_ref, qg_ref, w_uq_ref, w_uqs_ref, cosf_ref, sinf_ref,
                       yconv_ref, lat_ref, krope_ref, cstate_ref, q_ref, carry_ref):
    nb, ts, _ = x_ref.shape
    proj, _, _ = _pre_common(
        x_ref, mod_ref, cprev_ref, gmix_ref, w_in_ref, kvg_ref, convw_ref, convb_ref,
        cos128_ref, sin128_ref, yconv_ref, lat_ref, krope_ref, cstate_ref, carry_ref)
    cqn = _rms(proj[:, C_Q:C_Q + Q_LORA], qg_ref[...]).astype(BF16)
    q = (_dot(cqn, w_uq_ref[...]) * SM_SCALE).reshape(nb, ts, D_SLAB)
    qs = (_dot(cqn, w_uqs_ref[...]) * SM_SCALE).reshape(nb, ts, D_SLAB)
    q_ref[...] = (q * cosf_ref[...][None] + qs * sinf_ref[...][None]).astype(BF16)


def _pre_call(x, mod, cprev, w, tabs, *, nb, ts, tk, prompt):
    b, s, d = x.shape
    grid = (b // nb, s // ts)
    tok = lambda width: pl.BlockSpec((nb, ts, width), lambda i, j: (i, j, 0))
    per_b = lambda rows, width: pl.BlockSpec((nb, rows, width), lambda i, j: (i, 0, 0))
    in_specs = [tok(d), per_b(6, d), per_b(CONV_W - 1, D_CONV),
                _const_spec((1, d)), _const_spec((d, D_PROJ)), _const_spec((1, KV_LORA)),
                _const_spec((CONV_W, D_CONV)), _const_spec((1, D_CONV)),
                pl.BlockSpec((ts, HEAD_SLAB), lambda i, j: (j, 0)),
                pl.BlockSpec((ts, HEAD_SLAB), lambda i, j: (j, 0)),
                _const_spec((1, Q_LORA))]
    args = [x, mod, cprev, w["g_mix"], w["w_in"], w["kv_g"], w["conv_w"], w["conv_b"],
            tabs["cos128"], tabs["sin128"], w["q_g"]]
    out_specs = [tok(D_CONV), tok(KV_LORA), tok(QK_ROPE), per_b(CONV_W - 1, D_CONV)]
    out_shape = [jax.ShapeDtypeStruct((b, s, D_CONV), BF16),
                 jax.ShapeDtypeStruct((b, s, KV_LORA), F32),
                 jax.ShapeDtypeStruct((b, s, QK_ROPE), F32),
                 jax.ShapeDtypeStruct((b, CONV_W - 1, D_CONV), F32)]
    if prompt:
        assert nb == 1 and ts % tk == 0
        kernel = functools.partial(_pre_prompt_kernel, tk=tk)
        in_specs += [_const_spec((D_SLAB, Q_LORA)), _const_spec((KV_LORA, D_SLAB)),
                     _const_spec((D_ATTN, KV_LORA)),
                     pl.BlockSpec((HALF, ts), lambda i, j: (0, j)),
                     pl.BlockSpec((HALF, ts), lambda i, j: (0, j))]
        args += [w["w_uqT"], w["w_uk"], w["w_uvT"], tabs["cosT"], tabs["sinT"]]
        out_specs += [pl.BlockSpec((1, D_SLAB, ts), lambda i, j: (i, 0, j)),
                      tok(D_SLAB),
                      pl.BlockSpec((1, N_HEADS, ts // tk, V_ROWS, tk), lambda i, j: (i, 0, j, 0, 0))]
        out_shape += [jax.ShapeDtypeStruct((b, D_SLAB, s), BF16),
                      jax.ShapeDtypeStruct((b, s, D_SLAB), BF16),
                      jax.ShapeDtypeStruct((b, N_HEADS, s // tk, V_ROWS, tk), BF16)]
    else:
        assert ts == s
        kernel = _pre_sample_kernel
        in_specs += [_const_spec((Q_LORA, D_SLAB)), _const_spec((Q_LORA, D_SLAB)),
                     _const_spec((ts, D_SLAB)), _const_spec((ts, D_SLAB))]
        args += [w["w_uq"], w["w_uqs"], tabs["cosf"], tabs["sinf"]]
        out_specs += [tok(D_SLAB)]
        out_shape += [jax.ShapeDtypeStruct((b, s, D_SLAB), BF16)]
    return pl.pallas_call(
        kernel, grid=grid, in_specs=in_specs, out_specs=out_specs, out_shape=out_shape,
        scratch_shapes=[pltpu.VMEM((nb, ts + 8, D_CONV), F32)],
        compiler_params=pltpu.CompilerParams(dimension_semantics=("arbitrary", "arbitrary"),
                                             vmem_limit_bytes=VMEM_LIMIT_BYTES),
        name="pre_prompt" if prompt else "pre_sample",
    )(*args)


def _attn_kernel(qT_ref, k_ref, vT_ref, o_ref, s0_ref, s1_ref, mb0_ref, mb1_ref, acc_ref, m_ref,
                 *, tq, tk, heads):
    i = pl.program_id(2)
    m_ref[...] = jnp.full(m_ref.shape, NEG, F32)
    acc_ref[...] = jnp.zeros(acc_ref.shape, F32)

    def scores(j, s_ref, mb_ref, col0, masked):
        k0 = pl.multiple_of(j * tk, tk)
        for hh in range(heads):
            kb = k_ref[0, pl.ds(k0, tk), hh * HEAD_SLAB:(hh + 1) * HEAD_SLAB]
            qT = qT_ref[0, hh * HEAD_SLAB:(hh + 1) * HEAD_SLAB, col0:tq]
            sT = _dot(kb, qT)
            if masked:
                kpos = k0 + lax.broadcasted_iota(jnp.int32, (tk, 1), 0)
                qpos = i * tq + col0 + lax.broadcasted_iota(jnp.int32, (1, tq - col0), 1)
                sT = jnp.where(kpos <= (qpos | (CHUNK - 1)), sT, NEG)
            s_ref[hh, :, col0:tq] = sT
            mb_ref[hh, :, col0:tq] = jnp.max(sT, axis=0, keepdims=True)

    def fold(j, s_ref, mb_ref, col0):
        for hh in range(heads):
            m_old = m_ref[hh, :, col0:tq]
            m_new = jnp.maximum(m_old, mb_ref[hh, :, col0:tq])
            alpha = jnp.exp2(m_old - m_new)
            pT = jnp.exp2(s_ref[hh, :, col0:tq] - m_new).astype(BF16)
            pv = _dot(vT_ref[0, hh, j], pT)
            acc_ref[hh, :, col0:tq] = acc_ref[hh, :, col0:tq] * alpha + pv
            m_ref[hh, :, col0:tq] = m_new

    @pl.when(i > 0)
    def _():
        scores(0, s0_ref, mb0_ref, 0, False)

    def body(jj, carry):
        j = 2 * jj
        scores(j + 1, s1_ref, mb1_ref, 0, False)
        fold(j, s0_ref, mb0_ref, 0)
        scores(j + 2, s0_ref, mb0_ref, 0, False)
        fold(j + 1, s1_ref, mb1_ref, 0)
        return carry

    lax.fori_loop(0, i - 1, body, 0)

    @pl.when(i > 0)
    def _():
        j = 2 * i - 2
        scores(j + 1, s1_ref, mb1_ref, 0, False)
        fold(j, s0_ref, mb0_ref, 0)
        scores(j + 2, s0_ref, mb0_ref, 0, True)
        fold(j + 1, s1_ref, mb1_ref, 0)

    @pl.when(i == 0)
    def _():
        scores(0, s0_ref, mb0_ref, 0, True)

    jd = 2 * i
    scores(jd + 1, s1_ref, mb1_ref, tk, True)
    fold(jd, s0_ref, mb0_ref, 0)
    fold(jd + 1, s1_ref, mb1_ref, tk)

    outs = []
    for hh in range(heads):
        acc = acc_ref[hh]
        outs.append(acc[0:V_HEAD] / acc[V_HEAD:V_HEAD + 1])
    oT = jnp.concatenate(outs, axis=0)
    o_ref[0] = oT.T.astype(BF16)


def _attn_call(qT, k, vT, *, tq, tk, heads=4):
    b, _, s = qT.shape
    nkb = s // tk
    assert tk % CHUNK == 0 and tq == 2 * tk and s % tq == 0 and N_HEADS % heads == 0
    grid = (b, N_HEADS // heads, s // tq)
    return pl.pallas_call(
        functools.partial(_attn_kernel, tq=tq, tk=tk, heads=heads),
        grid=grid,
        in_specs=[pl.BlockSpec((1, heads * HEAD_SLAB, tq), lambda bi, h, i: (bi, h, i)),
                  pl.BlockSpec((1, s, heads * HEAD_SLAB), lambda bi, h, i: (bi, 0, h)),
                  pl.BlockSpec((1, heads, nkb, V_ROWS, tk), lambda bi, h, i: (bi, h, 0, 0, 0))],
        out_specs=pl.BlockSpec((1, tq, heads * V_HEAD), lambda bi, h, i: (bi, i, h)),
        out_shape=jax.ShapeDtypeStruct((b, s, D_ATTN), BF16),
        scratch_shapes=[pltpu.VMEM((heads, tk, tq), F32), pltpu.VMEM((heads, tk, tq), F32),
                        pltpu.VMEM((heads, 1, tq), F32), pltpu.VMEM((heads, 1, tq), F32),
                        pltpu.VMEM((heads, V_ROWS, tq), F32), pltpu.VMEM((heads, 1, tq), F32)],
        compiler_params=pltpu.CompilerParams(dimension_semantics=("arbitrary", "arbitrary", "arbitrary"),
                                             vmem_limit_bytes=VMEM_LIMIT_BYTES),
        name="attn_prompt",
    )(qT, k, vT)


def _sattn_kernel(q_ref, latp_ref, krp_ref, latn_ref, krn_ref, w_uk_ref, w_uv_ref, o_ref, o_scr):
    latp = latp_ref[0].astype(BF16)
    latn = latn_ref[0].astype(BF16)
    krp = krp_ref[0].astype(BF16)
    krn = krn_ref[0].astype(BF16)
    kp = _dot(latp, w_uk_ref[...]).astype(BF16)
    kn = _dot(latn, w_uk_ref[...]).astype(BF16)
    vp = _dot(latp, w_uv_ref[...]).astype(BF16)
    vn = _dot(latn, w_uv_ref[...]).astype(BF16)
    for h in range(N_HEADS):
        sl = slice(h * HEAD_SLAB, (h + 1) * HEAD_SLAB)
        qh = q_ref[0, :, sl]
        qr = qh[:, 0:QK_ROPE]
        s_p = _dot_nt(qh, kp[:, sl]) + _dot_nt(qr, krp)
        s_n = _dot_nt(qh, kn[:, sl]) + _dot_nt(qr, krn)
        m = jnp.maximum(jnp.max(s_p, axis=-1, keepdims=True), jnp.max(s_n, axis=-1, keepdims=True))
        p_p = jnp.exp(s_p - m)
        p_n = jnp.exp(s_n - m)
        l = jnp.sum(p_p, axis=-1, keepdims=True) + jnp.sum(p_n, axis=-1, keepdims=True)
        o = _dot((p_p / l).astype(BF16), vp[:, sl]) + _dot((p_n / l).astype(BF16), vn[:, sl])
        o_scr[:, h * V_HEAD:(h + 1) * V_HEAD] = o[:, 0:V_HEAD]
    o_ref[0] = o_scr[...].astype(BF16)


def _sattn_call(q, lat_past, kr_past, lat_new, kr_new, w_uk, w_uv):
    b, sd, _ = q.shape
    p = lat_past.shape[1]
    assert (p + sd - 1) // CHUNK <= p // CHUNK
    return pl.pallas_call(
        _sattn_kernel,
        grid=(b,),
        in_specs=[pl.BlockSpec((1, sd, D_SLAB), lambda i: (i, 0, 0)),
                  pl.BlockSpec((1, p, KV_LORA), lambda i: (i, 0, 0)),
                  pl.BlockSpec((1, p, QK_ROPE), lambda i: (i, 0, 0)),
                  pl.BlockSpec((1, sd, KV_LORA), lambda i: (i, 0, 0)),
                  pl.BlockSpec((1, sd, QK_ROPE), lambda i: (i, 0, 0)),
                  _const_spec((KV_LORA, D_SLAB)), _const_spec((KV_LORA, D_SLAB))],
        out_specs=pl.BlockSpec((1, sd, D_ATTN), lambda i: (i, 0, 0)),
        out_shape=jax.ShapeDtypeStruct((b, sd, D_ATTN), BF16),
        scratch_shapes=[pltpu.VMEM((sd, D_ATTN), F32)],
        compiler_params=pltpu.CompilerParams(dimension_semantics=("arbitrary",),
                                             vmem_limit_bytes=VMEM_LIMIT_BYTES),
        name="attn_sample",
    )(q, lat_past, kr_past, lat_new, kr_new, w_uk, w_uv)


def _post_kernel(x_ref, attn_ref, yconv_ref, mod_ref, w_oa_ref, w_oc_ref, gffn_ref,
                 w_gate_ref, w_up_ref, w_down_ref, gfin_ref, y_ref):
    nb, ts, d = x_ref.shape
    rows = nb * ts
    mod = mod_ref[...]
    a = attn_ref[...].reshape(rows, D_ATTN)
    yc = yconv_ref[...].reshape(rows, D_CONV)
    mix = _dot(a, w_oa_ref[...]) + _dot(yc, w_oc_ref[...])
    x1 = x_ref[...] + mod[:, 2:3, :] * mix.reshape(nb, ts, d)
    h = _rms(x1, gffn_ref[...]) * (1.0 + mod[:, 4:5, :]) + mod[:, 3:4, :]
    hb = h.reshape(rows, d).astype(BF16)
    gate = _dot(hb, w_gate_ref[...])
    up = _dot(hb, w_up_ref[...])
    act = ((gate * jax.nn.sigmoid(gate)) * up).astype(BF16)
    ff = _dot(act, w_down_ref[...])
    x2 = x1 + mod[:, 5:6, :] * ff.reshape(nb, ts, d)
    y_ref[...] = _rms(x2, gfin_ref[...])


def _post_call(x, attn, yconv, mod, w, *, nb, ts):
    b, s, d = x.shape
    dff = w["w_gate"].shape[1]
    tok = lambda width: pl.BlockSpec((nb, ts, width), lambda i, j: (i, j, 0))
    return pl.pallas_call(
        _post_kernel,
        grid=(b // nb, s // ts),
        in_specs=[tok(d), tok(D_ATTN), tok(D_CONV),
                  pl.BlockSpec((nb, 6, d), lambda i, j: (i, 0, 0)),
                  _const_spec((D_ATTN, d)), _const_spec((D_CONV, d)), _const_spec((1, d)),
                  _const_spec((d, dff)), _const_spec((d, dff)), _const_spec((dff, d)),
                  _const_spec((1, d))],
        out_specs=tok(d),
        out_shape=jax.ShapeDtypeStruct((b, s, d), F32),
        compiler_params=pltpu.CompilerParams(dimension_semantics=("arbitrary", "arbitrary"),
                                             vmem_limit_bytes=VMEM_LIMIT_BYTES),
        name="post",
    )(x, attn, yconv, mod, w["w_oa"], w["w_oc"], w["g_ffn"], w["w_gate"], w["w_up"], w["w_down"], w["g_fin"])


def _prep_weights(w_in, w_uq, w_ukv, w_out, norm_mix_g, q_norm_g, kv_norm_g, conv_w, conv_b,
                  norm_ffn_g, w_gate, w_up, w_down, final_norm_g):
    d = w_in.shape[0]
    i1 = Q_LORA
    i2 = i1 + KV_LORA
    i3 = i2 + QK_ROPE
    zpad = jnp.zeros((d, HEAD_SLAB - QK_ROPE), w_in.dtype)
    k_r = w_in[:, i2:i3]
    k_r_sw = jnp.concatenate([k_r[:, HALF:], k_r[:, :HALF]], axis=1)
    w_in_p = jnp.concatenate([w_in[:, :i2], k_r, zpad, k_r_sw, zpad, w_in[:, i3:]], axis=1)
    assert w_in_p.shape[1] == D_PROJ

    wq = w_uq.reshape(Q_LORA, N_HEADS, QK_NOPE + QK_ROPE)
    zq = jnp.zeros((Q_LORA, N_HEADS, HEAD_SLAB - QK_NOPE - QK_ROPE), w_uq.dtype)
    w_uq_p = jnp.concatenate([wq[..., QK_NOPE:], wq[..., :QK_NOPE], zq], axis=-1).reshape(Q_LORA, D_SLAB)
    zq2 = jnp.zeros((Q_LORA, N_HEADS, HEAD_SLAB - QK_ROPE), w_uq.dtype)
    w_uq_s = jnp.concatenate([wq[..., QK_NOPE + HALF:], wq[..., QK_NOPE:QK_NOPE + HALF], zq2],
                             axis=-1).reshape(Q_LORA, D_SLAB)

    wkv = w_ukv.reshape(KV_LORA, N_HEADS, QK_NOPE + V_HEAD)
    zk1 = jnp.zeros((KV_LORA, N_HEADS, QK_ROPE), w_ukv.dtype)
    zk2 = jnp.zeros((KV_LORA, N_HEADS, HEAD_SLAB - QK_ROPE - QK_NOPE), w_ukv.dtype)
    w_uk_p = jnp.concatenate([zk1, wkv[..., :QK_NOPE], zk2], axis=-1).reshape(KV_LORA, D_SLAB)
    w_uv = wkv[..., QK_NOPE:]
    zv = jnp.zeros((KV_LORA, N_HEADS, HEAD_SLAB - V_HEAD), w_ukv.dtype)
    w_uv_p = jnp.concatenate([w_uv, zv], axis=-1).reshape(KV_LORA, D_SLAB)
    w_uvT = w_uv.reshape(KV_LORA, D_ATTN).T

    return {
        "g_mix": norm_mix_g.reshape(1, -1), "w_in": w_in_p.astype(BF16),
        "q_g": q_norm_g.reshape(1, -1), "kv_g": kv_norm_g.reshape(1, -1),
        "conv_w": conv_w, "conv_b": conv_b.reshape(1, -1),
        "w_uq": w_uq_p.astype(BF16), "w_uqs": w_uq_s.astype(BF16), "w_uqT": w_uq_p.T.astype(BF16),
        "w_uk": w_uk_p.astype(BF16), "w_uv": w_uv_p.astype(BF16), "w_uvT": w_uvT.astype(BF16),
        "w_oa": w_out[:D_ATTN].astype(BF16), "w_oc": w_out[D_ATTN:].astype(BF16),
        "g_ffn": norm_ffn_g.reshape(1, -1),
        "w_gate": w_gate.astype(BF16), "w_up": w_up.astype(BF16), "w_down": w_down.astype(BF16),
        "g_fin": final_norm_g.reshape(1, -1),
    }


def _rope_tables(pos):
    n = pos.shape[0]
    inv_freq = ROPE_THETA ** (-jnp.arange(HALF, dtype=F32) / HALF)
    ang = pos.astype(F32)[:, None] * inv_freq[None, :]
    cos = jnp.cos(ang)
    sin = jnp.sin(ang)
    z96 = jnp.zeros((n, HEAD_SLAB - QK_ROPE), F32)
    cos128 = jnp.concatenate([cos, cos, z96], axis=1)
    sin128 = jnp.concatenate([-sin, sin, z96], axis=1)
    ones96 = jnp.ones((n, HEAD_SLAB - QK_ROPE), F32)
    cosf = jnp.tile(jnp.concatenate([cos, cos, ones96], axis=1), (1, N_HEADS))
    sinf = jnp.tile(sin128, (1, N_HEADS))
    return {"cos128": cos128, "sin128": sin128, "cosT": cos.T, "sinT": sin.T, "cosf": cosf, "sinf": sinf}


def kernel(x_prompt, x_sample, c_prompt, c_sample, cache_kv_latent, cache_k_rope, state_conv,
           w_ada, b_ada, norm_mix_g, w_in, q_norm_g, w_uq, kv_norm_g, w_ukv,
           conv_w, conv_b, w_out, norm_ffn_g, w_gate, w_up, w_down, final_norm_g):
    depth = w_in.shape[0]
    assert depth == 1
    bp, s, d = x_prompt.shape
    bs, sd, _ = x_sample.shape
    p = cache_kv_latent.shape[2]
    l = 0
    w = _prep_weights(w_in[l], w_uq[l], w_ukv[l], w_out[l], norm_mix_g[l], q_norm_g[l], kv_norm_g[l],
                      conv_w[l], conv_b[l], norm_ffn_g[l], w_gate[l], w_up[l], w_down[l], final_norm_g)
    tabs_p = _rope_tables(jnp.arange(s, dtype=jnp.int32))
    tabs_s = _rope_tables(p + jnp.arange(sd, dtype=jnp.int32))

    mod = _mod_call(jnp.concatenate([c_prompt, c_sample], axis=0), w_ada[l], b_ada[l])
    mod = mod.reshape(bp + bs, 6, d)
    mod_p, mod_s = mod[:bp], mod[bp:]

    tq, tk = 512, 256
    conv_zero = jnp.zeros((bp, CONV_W - 1, D_CONV), x_prompt.dtype)
    yconv_p, lat_p, kr_p, cv_p, qT, k_all, vT = _pre_call(
        x_prompt, mod_p, conv_zero, w, tabs_p, nb=1, ts=512, tk=tk, prompt=True)
    attn_p = _attn_call(qT, k_all, vT, tq=tq, tk=tk)
    y_prompt = _post_call(x_prompt, attn_p, yconv_p, mod_p, w, nb=1, ts=512)

    nb_s = 8
    yconv_s, lat_s, kr_s, cv_s, q_s = _pre_call(
        x_sample, mod_s, state_conv[l], w, tabs_s, nb=nb_s, ts=sd, tk=tk, prompt=False)
    attn_s = _sattn_call(q_s, cache_kv_latent[l], cache_k_rope[l], lat_s, kr_s, w["w_uk"], w["w_uv"])
    y_sample = _post_call(x_sample, attn_s, yconv_s, mod_s, w, nb=nb_s, ts=sd)

    return (y_prompt, y_sample, lat_p[None], kr_p[None], cv_p[None], lat_s[None], kr_s[None], cv_s[None])
```

```python
import functools

import jax
import jax.numpy as jnp
from jax import lax
from jax.experimental import pallas as pl
from jax.experimental.pallas import tpu as pltpu

CHUNK = 64
N_HEADS = 8
QK_NOPE = 64
QK_ROPE = 32
V_HEAD = 64
KV_LORA = 256
Q_LORA = 768
D_CONV = 512
CONV_W = 3
ROPE_THETA = 10000.0
EPS = 1e-6
SM_SCALE = (QK_NOPE + QK_ROPE) ** -0.5
LOG2E = 1.4426950408889634

HALF = QK_ROPE // 2
HEAD_SLAB = 128
D_SLAB = N_HEADS * HEAD_SLAB
D_ATTN = N_HEADS * V_HEAD
V_ROWS = V_HEAD + 16

C_Q = 0
C_KV = C_Q + Q_LORA
C_KR = C_KV + KV_LORA
C_KRS = C_KR + HEAD_SLAB
C_U = C_KRS + HEAD_SLAB
C_GB = C_U + D_CONV
C_GC = C_GB + D_CONV
D_PROJ = C_GC + D_CONV

VMEM_LIMIT_BYTES = 56 * 1024 * 1024

BF16 = jnp.bfloat16
F32 = jnp.float32
NEG = float(jnp.finfo(jnp.float32).min)


def _rms(x, g):
    ms = jnp.mean(x * x, axis=-1, keepdims=True)
    return x * lax.rsqrt(ms + EPS) * g


def _dot(a, b):
    return jnp.dot(a, b, preferred_element_type=F32)


def _dot_nt(a, b):
    return lax.dot_general(a, b, (((1,), (1,)), ((), ())), preferred_element_type=F32)


def _const_spec(shape):
    nd = len(shape)
    return pl.BlockSpec(shape, lambda *_: (0,) * nd, pipeline_mode=pl.Buffered(1))


def _mod_kernel(c_ref, w_ref, b_ref, o_ref):
    c = c_ref[...]
    a = (c * jax.nn.sigmoid(c)).astype(BF16)
    o_ref[...] = _dot(a, w_ref[...]) + b_ref[...]


def _mod_call(c_all, w_ada, b_ada):
    n, d = c_all.shape
    dn = w_ada.shape[1]
    tn = dn // 4
    return pl.pallas_call(
        _mod_kernel,
        grid=(dn // tn,),
        in_specs=[pl.BlockSpec((n, d), lambda j: (0, 0)),
                  pl.BlockSpec((d, tn), lambda j: (0, j)),
                  pl.BlockSpec((1, tn), lambda j: (0, j))],
        out_specs=pl.BlockSpec((n, tn), lambda j: (0, j)),
        out_shape=jax.ShapeDtypeStruct((n, dn), F32),
        compiler_params=pltpu.CompilerParams(dimension_semantics=("arbitrary",),
                                             vmem_limit_bytes=VMEM_LIMIT_BYTES),
        name="mod",
    )(c_all, w_ada.astype(BF16), b_ada.reshape(1, dn))


def _pre_common(x_ref, mod_ref, cprev_ref, gmix_ref, w_in_ref, kvg_ref, convw_ref, convb_ref,
                cos128_ref, sin128_ref, yconv_ref, lat_ref, krope_ref, cstate_ref, carry_ref):
    nb, ts, d = x_ref.shape
    rows = nb * ts
    x = x_ref[...]
    mod = mod_ref[...]
    h = _rms(x, gmix_ref[...]) * (1.0 + mod[:, 1:2, :]) + mod[:, 0:1, :]
    proj = _dot(h.reshape(rows, d).astype(BF16), w_in_ref[...])

    latent = _rms(proj[:, C_KV:C_KV + KV_LORA], kvg_ref[...])
    lat_ref[...] = latent.reshape(nb, ts, KV_LORA)

    g1 = proj[:, C_KR:C_KR + HEAD_SLAB].reshape(nb, ts, HEAD_SLAB)
    g2 = proj[:, C_KRS:C_KRS + HEAD_SLAB].reshape(nb, ts, HEAD_SLAB)
    kr128 = g1 * cos128_ref[...][None] + g2 * sin128_ref[...][None]
    krope_ref[...] = kr128[:, :, :QK_ROPE]

    u = proj[:, C_U:C_U + D_CONV]
    g_b = proj[:, C_GB:C_GB + D_CONV].reshape(nb, ts, D_CONV)
    g_c = proj[:, C_GC:C_GC + D_CONV]
    gated = (g_c * u).reshape(nb, ts, D_CONV)

    @pl.when(pl.program_id(1) == 0)
    def _():
        carry_ref[:, 6:8, :] = cprev_ref[...]

    carry_ref[:, 8:8 + ts, :] = gated
    prev2 = carry_ref[:, 6:6 + ts, :]
    prev1 = carry_ref[:, 7:7 + ts, :]
    cw = convw_ref[...]
    conv = cw[0:1][None] * prev2 + cw[1:2][None] * prev1 + cw[2:3][None] * gated + convb_ref[...][None]
    yconv_ref[...] = (g_b * conv).astype(BF16)
    state = gated[:, ts - 2:ts, :]
    cstate_ref[...] = state
    carry_ref[:, 6:8, :] = state
    return proj, latent.astype(BF16), kr128


def _pre_prompt_kernel(x_ref, mod_ref, cprev_ref, gmix_ref, w_in_ref, kvg_ref, convw_ref, convb_ref,
                       cos128_ref, sin128_ref, qg_ref, w_uqT_ref, w_uk_ref, w_uvT_ref, cosT_ref, sinT_ref,
                       yconv_ref, lat_ref, krope_ref, cstate_ref, qT_ref, k_ref, vT_ref,
                       carry_ref, *, tk):
    nb, ts, _ = x_ref.shape
    proj, lat_bf, kr128 = _pre_common(
        x_ref, mod_ref, cprev_ref, gmix_ref, w_in_ref, kvg_ref, convw_ref, convb_ref,
        cos128_ref, sin128_ref, yconv_ref, lat_ref, krope_ref, cstate_ref, carry_ref)
    kr128 = kr128.reshape(ts, HEAD_SLAB)

    cqn = _rms(proj[:, C_Q:C_Q + Q_LORA], qg_ref[...]).astype(BF16)
    qT = _dot_nt(w_uqT_ref[...], cqn) * (SM_SCALE * LOG2E)
    c = cosT_ref[...]
    s = sinT_ref[...]
    for h in range(N_HEADS):
        b0 = h * HEAD_SLAB
        x1 = qT[b0:b0 + HALF]
        x2 = qT[b0 + HALF:b0 + QK_ROPE]
        qT_ref[0, b0:b0 + HALF, :] = (x1 * c - x2 * s).astype(BF16)
        qT_ref[0, b0 + HALF:b0 + QK_ROPE, :] = (x1 * s + x2 * c).astype(BF16)
        qT_ref[0, b0 + QK_ROPE:b0 + HEAD_SLAB, :] = qT[b0 + QK_ROPE:b0 + HEAD_SLAB].astype(BF16)

    k_nope = _dot(lat_bf, w_uk_ref[...])
    for h in range(N_HEADS):
        b0 = h * HEAD_SLAB
        k_ref[0, :, b0:b0 + HEAD_SLAB] = (k_nope[:, b0:b0 + HEAD_SLAB] + kr128).astype(BF16)

    vT = _dot_nt(w_uvT_ref[...], lat_bf)
    row = lax.broadcasted_iota(jnp.int32, (V_ROWS - V_HEAD, tk), 0)
    ones_rows = jnp.where(row == 0, 1.0, 0.0).astype(BF16)
    for h in range(N_HEADS):
        for cb in range(ts // tk):
            vT_ref[0, h, cb, 0:V_HEAD, :] = vT[h * V_HEAD:(h + 1) * V_HEAD, cb * tk:(cb + 1) * tk].astype(BF16)
            vT_ref[0, h, cb, V_HEAD:V_ROWS, :] = ones_rows


def _pre_sample_kernel(x_ref, mod_ref, cprev_ref, gmix_ref, w_in_ref, kvg_ref, convw_ref, convb_ref,
                       cos128_ref, sin128_ref, qg_ref, w_uq_ref, w_uqs_ref, cosf_ref, sinf_ref,
                       yconv_ref, lat_ref, krope_ref, cstate_ref, q_ref, carry_ref):
    nb, ts, _ = x_ref.shape
    proj, _, _ = _pre_common(
        x_ref, mod_ref, cprev_ref, gmix_ref, w_in_ref, kvg_ref, convw_ref, convb_ref,
        cos128_ref, sin128_ref, yconv_ref, lat_ref, krope_ref, cstate_ref, carry_ref)
    cqn = _rms(proj[:, C_Q:C_Q + Q_LORA], qg_ref[...]).astype(BF16)
    q = (_dot(cqn, w_uq_ref[...]) * SM_SCALE).reshape(nb, ts, D_SLAB)
    qs = (_dot(cqn, w_uqs_ref[...]) * SM_SCALE).reshape(nb, ts, D_SLAB)
    q_ref[...] = (q * cosf_ref[...][None] + qs * sinf_ref[...][None]).astype(BF16)


def _pre_call(x, mod, cprev, w, tabs, *, nb, ts, tk, prompt):
    b, s, d = x.shape
    grid = (b // nb, s // ts)
    tok = lambda width: pl.BlockSpec((nb, ts, width), lambda i, j: (i, j, 0))
    per_b = lambda rows, width: pl.BlockSpec((nb, rows, width), lambda i, j: (i, 0, 0))
    in_specs = [tok(d), per_b(6, d), per_b(CONV_W - 1, D_CONV),
                _const_spec((1, d)), _const_spec((d, D_PROJ)), _const_spec((1, KV_LORA)),
                _const_spec((CONV_W, D_CONV)), _const_spec((1, D_CONV)),
                pl.BlockSpec((ts, HEAD_SLAB), lambda i, j: (j, 0)),
                pl.BlockSpec((ts, HEAD_SLAB), lambda i, j: (j, 0)),
                _const_spec((1, Q_LORA))]
    args = [x, mod, cprev, w["g_mix"], w["w_in"], w["kv_g"], w["conv_w"], w["conv_b"],
            tabs["cos128"], tabs["sin128"], w["q_g"]]
    out_specs = [tok(D_CONV), tok(KV_LORA), tok(QK_ROPE), per_b(CONV_W - 1, D_CONV)]
    out_shape = [jax.ShapeDtypeStruct((b, s, D_CONV), BF16),
                 jax.ShapeDtypeStruct((b, s, KV_LORA), F32),
                 jax.ShapeDtypeStruct((b, s, QK_ROPE), F32),
                 jax.ShapeDtypeStruct((b, CONV_W - 1, D_CONV), F32)]
    if prompt:
        assert nb == 1 and ts % tk == 0
        kernel = functools.partial(_pre_prompt_kernel, tk=tk)
        in_specs += [_const_spec((D_SLAB, Q_LORA)), _const_spec((KV_LORA, D_SLAB)),
                     _const_spec((D_ATTN, KV_LORA)),
                     pl.BlockSpec((HALF, ts), lambda i, j: (0, j)),
                     pl.BlockSpec((HALF, ts), lambda i, j: (0, j))]
        args += [w["w_uqT"], w["w_uk"], w["w_uvT"], tabs["cosT"], tabs["sinT"]]
        out_specs += [pl.BlockSpec((1, D_SLAB, ts), lambda i, j: (i, 0, j)),
                      tok(D_SLAB),
                      pl.BlockSpec((1, N_HEADS, ts // tk, V_ROWS, tk), lambda i, j: (i, 0, j, 0, 0))]
        out_shape += [jax.ShapeDtypeStruct((b, D_SLAB, s), BF16),
                      jax.ShapeDtypeStruct((b, s, D_SLAB), BF16),
                      jax.ShapeDtypeStruct((b, N_HEADS, s // tk, V_ROWS, tk), BF16)]
    else:
        assert ts == s
        kernel = _pre_sample_kernel
        in_specs += [_const_spec((Q_LORA, D_SLAB)), _const_spec((Q_LORA, D_SLAB)),
                     _const_spec((ts, D_SLAB)), _const_spec((ts, D_SLAB))]
        args += [w["w_uq"], w["w_uqs"], tabs["cosf"], tabs["sinf"]]
        out_specs += [tok(D_SLAB)]
        out_shape += [jax.ShapeDtypeStruct((b, s, D_SLAB), BF16)]
    return pl.pallas_call(
        kernel, grid=grid, in_specs=in_specs, out_specs=out_specs, out_shape=out_shape,
        scratch_shapes=[pltpu.VMEM((nb, ts + 8, D_CONV), F32)],
        compiler_params=pltpu.CompilerParams(dimension_semantics=("arbitrary", "arbitrary"),
                                             vmem_limit_bytes=VMEM_LIMIT_BYTES),
        name="pre_prompt" if prompt else "pre_sample",
    )(*args)


def _attn_kernel(qT_ref, k_ref, vT_ref, o_ref, s0_ref, s1_ref, mb0_ref, mb1_ref, acc_ref, m_ref,
                 *, tq, tk, heads):
    i = pl.program_id(2)
    m_ref[...] = jnp.full(m_ref.shape, NEG, F32)
    acc_ref[...] = jnp.zeros(acc_ref.shape, F32)
    even = (s0_ref, mb0_ref)
    odd = (s1_ref, mb1_ref)

    def score_head(j, dst, hh, masked, col0=0):
        s_ref, mb_ref = dst
        k0 = pl.multiple_of(j * tk, tk)
        kb = k_ref[0, pl.ds(k0, tk), hh * HEAD_SLAB:(hh + 1) * HEAD_SLAB]
        qT = qT_ref[0, hh * HEAD_SLAB:(hh + 1) * HEAD_SLAB, col0:tq]
        sT = _dot(kb, qT)
        if masked:
            kpos = k0 + lax.broadcasted_iota(jnp.int32, (tk, 1), 0)
            qpos = i * tq + col0 + lax.broadcasted_iota(jnp.int32, (1, tq - col0), 1)
            sT = jnp.where(kpos <= (qpos | (CHUNK - 1)), sT, NEG)
        s_ref[hh, :, col0:tq] = sT
        mb_ref[hh, :, col0:tq] = jnp.max(sT, axis=0, keepdims=True)

    def fold_head(j, src, hh, col0=0):
        s_ref, mb_ref = src
        m_old = m_ref[hh, :, col0:tq]
        m_new = jnp.maximum(m_old, mb_ref[hh, :, col0:tq])
        alpha = jnp.exp2(m_old - m_new)
        pT = jnp.exp2(s_ref[hh, :, col0:tq] - m_new).astype(BF16)
        pv = _dot(vT_ref[0, hh, j], pT)
        acc_ref[hh, :, col0:tq] = acc_ref[hh, :, col0:tq] * alpha + pv
        m_ref[hh, :, col0:tq] = m_new

    def stage(j, src, dst, masked, col0=0):
        for hh in range(heads):
            score_head(j + 1, dst, hh, masked, col0)
            fold_head(j, src, hh)

    jd = 2 * i

    @pl.when(i > 0)
    def _():
        for hh in range(heads):
            score_head(0, even, hh, False)

    @pl.when(i == 0)
    def _():
        for hh in range(heads):
            score_head(0, even, hh, True)

    def body(jj, carry):
        j = 2 * jj
        stage(j, even, odd, False)
        stage(j + 1, odd, even, False)
        return carry

    lax.fori_loop(0, i - 1, body, 0)

    @pl.when(i > 0)
    def _():
        stage(jd - 2, even, odd, False)
        stage(jd - 1, odd, even, True)

    stage(jd, even, odd, True, tk)
    for hh in range(heads):
        fold_head(jd + 1, odd, hh, tk)

    outs = []
    for hh in range(heads):
        acc = acc_ref[hh]
        outs.append(acc[0:V_HEAD] / acc[V_HEAD:V_HEAD + 1])
    oT = jnp.concatenate(outs, axis=0)
    o_ref[0] = oT.T.astype(BF16)


def _attn_call(qT, k, vT, *, tq, tk, heads=N_HEADS):
    b, _, s = qT.shape
    nkb = s // tk
    assert tk % CHUNK == 0 and tq == 2 * tk and s % tq == 0 and N_HEADS % heads == 0
    grid = (b, N_HEADS // heads, s // tq)
    once = pl.Buffered(1)
    return pl.pallas_call(
        functools.partial(_attn_kernel, tq=tq, tk=tk, heads=heads),
        grid=grid,
        in_specs=[pl.BlockSpec((1, heads * HEAD_SLAB, tq), lambda bi, h, i: (bi, h, i)),
                  pl.BlockSpec((1, s, heads * HEAD_SLAB), lambda bi, h, i: (bi, 0, h), pipeline_mode=once),
                  pl.BlockSpec((1, heads, nkb, V_ROWS, tk), lambda bi, h, i: (bi, h, 0, 0, 0),
                               pipeline_mode=once)],
        out_specs=pl.BlockSpec((1, tq, heads * V_HEAD), lambda bi, h, i: (bi, i, h)),
        out_shape=jax.ShapeDtypeStruct((b, s, D_ATTN), BF16),
        scratch_shapes=[pltpu.VMEM((heads, tk, tq), F32), pltpu.VMEM((heads, tk, tq), F32),
                        pltpu.VMEM((heads, 1, tq), F32), pltpu.VMEM((heads, 1, tq), F32),
                        pltpu.VMEM((heads, V_ROWS, tq), F32), pltpu.VMEM((heads, 1, tq), F32)],
        compiler_params=pltpu.CompilerParams(dimension_semantics=("arbitrary", "arbitrary", "arbitrary"),
                                             vmem_limit_bytes=VMEM_LIMIT_BYTES),
        name="attn_prompt",
    )(qT, k, vT)


def _sattn_kernel(q_ref, latp_ref, krp_ref, latn_ref, krn_ref, w_ukT_ref, w_uv_ref, o_ref, o_scr):
    sd = q_ref.shape[1]
    latp = latp_ref[0].astype(BF16)
    latn = latn_ref[0].astype(BF16)
    krp = krp_ref[0].astype(BF16)
    krn = krn_ref[0].astype(BF16)
    qa, qr = [], []
    for h in range(N_HEADS):
        qh = q_ref[0, :, h * HEAD_SLAB:(h + 1) * HEAD_SLAB]
        qa.append(_dot(qh, w_ukT_ref[h]).astype(BF16))
        qr.append(qh[:, 0:QK_ROPE])
    qa = jnp.concatenate(qa, axis=0)
    qr = jnp.concatenate(qr, axis=0)
    s_p = _dot_nt(qa, latp) + _dot_nt(qr, krp)
    s_n = _dot_nt(qa, latn) + _dot_nt(qr, krn)
    m = jnp.maximum(jnp.max(s_p, axis=-1, keepdims=True), jnp.max(s_n, axis=-1, keepdims=True))
    p_p = jnp.exp(s_p - m)
    p_n = jnp.exp(s_n - m)
    l = jnp.sum(p_p, axis=-1, keepdims=True) + jnp.sum(p_n, axis=-1, keepdims=True)
    o_lat = (_dot(p_p.astype(BF16), latp) + _dot(p_n.astype(BF16), latn)) / l
    o_all = _dot(o_lat.astype(BF16), w_uv_ref[...])
    for h in range(N_HEADS):
        o_scr[:, h * V_HEAD:(h + 1) * V_HEAD] = o_all[h * sd:(h + 1) * sd, h * V_HEAD:(h + 1) * V_HEAD]
    o_ref[0] = o_scr[...].astype(BF16)


def _sattn_call(q, lat_past, kr_past, lat_new, kr_new, w_ukT, w_uv):
    b, sd, _ = q.shape
    p = lat_past.shape[1]
    assert (p + sd - 1) // CHUNK <= p // CHUNK
    return pl.pallas_call(
        _sattn_kernel,
        grid=(b,),
        in_specs=[pl.BlockSpec((1, sd, D_SLAB), lambda i: (i, 0, 0)),
                  pl.BlockSpec((1, p, KV_LORA), lambda i: (i, 0, 0)),
                  pl.BlockSpec((1, p, QK_ROPE), lambda i: (i, 0, 0)),
                  pl.BlockSpec((1, sd, KV_LORA), lambda i: (i, 0, 0)),
                  pl.BlockSpec((1, sd, QK_ROPE), lambda i: (i, 0, 0)),
                  _const_spec((N_HEADS, HEAD_SLAB, KV_LORA)), _const_spec((KV_LORA, D_ATTN))],
        out_specs=pl.BlockSpec((1, sd, D_ATTN), lambda i: (i, 0, 0)),
        out_shape=jax.ShapeDtypeStruct((b, sd, D_ATTN), BF16),
        scratch_shapes=[pltpu.VMEM((sd, D_ATTN), F32)],
        compiler_params=pltpu.CompilerParams(dimension_semantics=("arbitrary",),
                                             vmem_limit_bytes=VMEM_LIMIT_BYTES),
        name="attn_sample",
    )(q, lat_past, kr_past, lat_new, kr_new, w_ukT, w_uv)


def _post_kernel(x_ref, attn_ref, yconv_ref, mod_ref, w_oa_ref, w_oc_ref, gffn_ref,
                 w_gate_ref, w_up_ref, w_down_ref, gfin_ref, y_ref):
    nb, ts, d = x_ref.shape
    rows = nb * ts
    mod = mod_ref[...]
    a = attn_ref[...].reshape(rows, D_ATTN)
    yc = yconv_ref[...].reshape(rows, D_CONV)
    mix = _dot(a, w_oa_ref[...]) + _dot(yc, w_oc_ref[...])
    x1 = x_ref[...] + mod[:, 2:3, :] * mix.reshape(nb, ts, d)
    h = _rms(x1, gffn_ref[...]) * (1.0 + mod[:, 4:5, :]) + mod[:, 3:4, :]
    hb = h.reshape(rows, d).astype(BF16)
    gate = _dot(hb, w_gate_ref[...])
    up = _dot(hb, w_up_ref[...])
    act = ((gate * jax.nn.sigmoid(gate)) * up).astype(BF16)
    ff = _dot(act, w_down_ref[...])
    x2 = x1 + mod[:, 5:6, :] * ff.reshape(nb, ts, d)
    y_ref[...] = _rms(x2, gfin_ref[...])


def _post_call(x, attn, yconv, mod, w, *, nb, ts):
    b, s, d = x.shape
    dff = w["w_gate"].shape[1]
    tok = lambda width: pl.BlockSpec((nb, ts, width), lambda i, j: (i, j, 0))
    return pl.pallas_call(
        _post_kernel,
        grid=(b // nb, s // ts),
        in_specs=[tok(d), tok(D_ATTN), tok(D_CONV),
                  pl.BlockSpec((nb, 6, d), lambda i, j: (i, 0, 0)),
                  _const_spec((D_ATTN, d)), _const_spec((D_CONV, d)), _const_spec((1, d)),
                  _const_spec((d, dff)), _const_spec((d, dff)), _const_spec((dff, d)),
                  _const_spec((1, d))],
        out_specs=tok(d),
        out_shape=jax.ShapeDtypeStruct((b, s, d), F32),
        compiler_params=pltpu.CompilerParams(dimension_semantics=("arbitrary", "arbitrary"),
                                             vmem_limit_bytes=VMEM_LIMIT_BYTES),
        name="post",
    )(x, attn, yconv, mod, w["w_oa"], w["w_oc"], w["g_ffn"], w["w_gate"], w["w_up"], w["w_down"], w["g_fin"])


def _prep_weights(w_in, w_uq, w_ukv, w_out, norm_mix_g, q_norm_g, kv_norm_g, conv_w, conv_b,
                  norm_ffn_g, w_gate, w_up, w_down, final_norm_g):
    d = w_in.shape[0]
    i1 = Q_LORA
    i2 = i1 + KV_LORA
    i3 = i2 + QK_ROPE
    zpad = jnp.zeros((d, HEAD_SLAB - QK_ROPE), w_in.dtype)
    k_r = w_in[:, i2:i3]
    k_r_sw = jnp.concatenate([k_r[:, HALF:], k_r[:, :HALF]], axis=1)
    w_in_p = jnp.concatenate([w_in[:, :i2], k_r, zpad, k_r_sw, zpad, w_in[:, i3:]], axis=1)
    assert w_in_p.shape[1] == D_PROJ

    wq = w_uq.reshape(Q_LORA, N_HEADS, QK_NOPE + QK_ROPE)
    zq = jnp.zeros((Q_LORA, N_HEADS, HEAD_SLAB - QK_NOPE - QK_ROPE), w_uq.dtype)
    w_uq_p = jnp.concatenate([wq[..., QK_NOPE:], wq[..., :QK_NOPE], zq], axis=-1).reshape(Q_LORA, D_SLAB)
    zq2 = jnp.zeros((Q_LORA, N_HEADS, HEAD_SLAB - QK_ROPE), w_uq.dtype)
    w_uq_s = jnp.concatenate([wq[..., QK_NOPE + HALF:], wq[..., QK_NOPE:QK_NOPE + HALF], zq2],
                             axis=-1).reshape(Q_LORA, D_SLAB)

    wkv = w_ukv.reshape(KV_LORA, N_HEADS, QK_NOPE + V_HEAD)
    zk1 = jnp.zeros((KV_LORA, N_HEADS, QK_ROPE), w_ukv.dtype)
    zk2 = jnp.zeros((KV_LORA, N_HEADS, HEAD_SLAB - QK_ROPE - QK_NOPE), w_ukv.dtype)
    w_uk_p = jnp.concatenate([zk1, wkv[..., :QK_NOPE], zk2], axis=-1).reshape(KV_LORA, D_SLAB)
    w_uv = wkv[..., QK_NOPE:].reshape(KV_LORA, D_ATTN)
    w_ukT = w_uk_p.T.reshape(N_HEADS, HEAD_SLAB, KV_LORA)

    return {
        "g_mix": norm_mix_g.reshape(1, -1), "w_in": w_in_p.astype(BF16),
        "q_g": q_norm_g.reshape(1, -1), "kv_g": kv_norm_g.reshape(1, -1),
        "conv_w": conv_w, "conv_b": conv_b.reshape(1, -1),
        "w_uq": w_uq_p.astype(BF16), "w_uqs": w_uq_s.astype(BF16), "w_uqT": w_uq_p.T.astype(BF16),
        "w_uk": w_uk_p.astype(BF16), "w_ukT": w_ukT.astype(BF16),
        "w_uv": w_uv.astype(BF16), "w_uvT": w_uv.T.astype(BF16),
        "w_oa": w_out[:D_ATTN].astype(BF16), "w_oc": w_out[D_ATTN:].astype(BF16),
        "g_ffn": norm_ffn_g.reshape(1, -1),
        "w_gate": w_gate.astype(BF16), "w_up": w_up.astype(BF16), "w_down": w_down.astype(BF16),
        "g_fin": final_norm_g.reshape(1, -1),
    }


def _rope_tables(pos):
    n = pos.shape[0]
    inv_freq = ROPE_THETA ** (-jnp.arange(HALF, dtype=F32) / HALF)
    ang = pos.astype(F32)[:, None] * inv_freq[None, :]
    cos = jnp.cos(ang)
    sin = jnp.sin(ang)
    z96 = jnp.zeros((n, HEAD_SLAB - QK_ROPE), F32)
    cos128 = jnp.concatenate([cos, cos, z96], axis=1)
    sin128 = jnp.concatenate([-sin, sin, z96], axis=1)
    ones96 = jnp.ones((n, HEAD_SLAB - QK_ROPE), F32)
    cosf = jnp.tile(jnp.concatenate([cos, cos, ones96], axis=1), (1, N_HEADS))
    sinf = jnp.tile(sin128, (1, N_HEADS))
    return {"cos128": cos128, "sin128": sin128, "cosT": cos.T, "sinT": sin.T, "cosf": cosf, "sinf": sinf}


def kernel(x_prompt, x_sample, c_prompt, c_sample, cache_kv_latent, cache_k_rope, state_conv,
           w_ada, b_ada, norm_mix_g, w_in, q_norm_g, w_uq, kv_norm_g, w_ukv,
           conv_w, conv_b, w_out, norm_ffn_g, w_gate, w_up, w_down, final_norm_g):
    depth = w_in.shape[0]
    assert depth == 1
    bp, s, d = x_prompt.shape
    bs, sd, _ = x_sample.shape
    p = cache_kv_latent.shape[2]
    l = 0
    w = _prep_weights(w_in[l], w_uq[l], w_ukv[l], w_out[l], norm_mix_g[l], q_norm_g[l], kv_norm_g[l],
                      conv_w[l], conv_b[l], norm_ffn_g[l], w_gate[l], w_up[l], w_down[l], final_norm_g)
    tabs_p = _rope_tables(jnp.arange(s, dtype=jnp.int32))
    tabs_s = _rope_tables(p + jnp.arange(sd, dtype=jnp.int32))

    mod = _mod_call(jnp.concatenate([c_prompt, c_sample], axis=0), w_ada[l], b_ada[l])
    mod = mod.reshape(bp + bs, 6, d)
    mod_p, mod_s = mod[:bp], mod[bp:]

    tq, tk = 512, 256
    conv_zero = jnp.zeros((bp, CONV_W - 1, D_CONV), x_prompt.dtype)
    yconv_p, lat_p, kr_p, cv_p, qT, k_all, vT = _pre_call(
        x_prompt, mod_p, conv_zero, w, tabs_p, nb=1, ts=512, tk=tk, prompt=True)
    attn_p = _attn_call(qT, k_all, vT, tq=tq, tk=tk)
    y_prompt = _post_call(x_prompt, attn_p, yconv_p, mod_p, w, nb=1, ts=512)

    nb_s = 8
    yconv_s, lat_s, kr_s, cv_s, q_s = _pre_call(
        x_sample, mod_s, state_conv[l], w, tabs_s, nb=nb_s, ts=sd, tk=tk, prompt=False)
    attn_s = _sattn_call(q_s, cache_kv_latent[l], cache_k_rope[l], lat_s, kr_s, w["w_ukT"], w["w_uv"])
    y_sample = _post_call(x_sample, attn_s, yconv_s, mod_s, w, nb=nb_s, ts=sd)

    return (y_prompt, y_sample, lat_p[None], kr_p[None], cv_p[None], lat_s[None], kr_s[None], cv_s[None])
```

```python
import functools

import jax
import jax.numpy as jnp
from jax import lax
from jax.experimental import pallas as pl
from jax.experimental.pallas import tpu as pltpu

CHUNK = 64
N_HEADS = 8
QK_NOPE = 64
QK_ROPE = 32
V_HEAD = 64
KV_LORA = 256
Q_LORA = 768
D_CONV = 512
CONV_W = 3
ROPE_THETA = 10000.0
EPS = 1e-6
SM_SCALE = (QK_NOPE + QK_ROPE) ** -0.5
LOG2E = 1.4426950408889634

HALF = QK_ROPE // 2
HEAD_SLAB = 128
D_SLAB = N_HEADS * HEAD_SLAB
D_ATTN = N_HEADS * V_HEAD
V_ROWS = V_HEAD + 16

C_Q = 0
C_KV = C_Q + Q_LORA
C_KR = C_KV + KV_LORA
C_KRS = C_KR + HEAD_SLAB
C_U = C_KRS + HEAD_SLAB
C_GB = C_U + D_CONV
C_GC = C_GB + D_CONV
D_PROJ = C_GC + D_CONV

VMEM_LIMIT_BYTES = 56 * 1024 * 1024

BF16 = jnp.bfloat16
F32 = jnp.float32
NEG = float(jnp.finfo(jnp.float32).min)


def _rms(x, g):
    ms = jnp.mean(x * x, axis=-1, keepdims=True)
    return x * lax.rsqrt(ms + EPS) * g


def _dot(a, b):
    return jnp.dot(a, b, preferred_element_type=F32)


def _dot_nt(a, b):
    return lax.dot_general(a, b, (((1,), (1,)), ((), ())), preferred_element_type=F32)


def _const_spec(shape):
    nd = len(shape)
    return pl.BlockSpec(shape, lambda *_: (0,) * nd, pipeline_mode=pl.Buffered(1))


def _mod_kernel(c_ref, w_ref, b_ref, o_ref):
    c = c_ref[...]
    a = (c * jax.nn.sigmoid(c)).astype(BF16)
    o_ref[...] = _dot(a, w_ref[...]) + b_ref[...]


def _mod_call(c_all, w_ada, b_ada):
    n, d = c_all.shape
    dn = w_ada.shape[1]
    tn = dn // 4
    return pl.pallas_call(
        _mod_kernel,
        grid=(dn // tn,),
        in_specs=[pl.BlockSpec((n, d), lambda j: (0, 0)),
                  pl.BlockSpec((d, tn), lambda j: (0, j)),
                  pl.BlockSpec((1, tn), lambda j: (0, j))],
        out_specs=pl.BlockSpec((n, tn), lambda j: (0, j)),
        out_shape=jax.ShapeDtypeStruct((n, dn), F32),
        compiler_params=pltpu.CompilerParams(dimension_semantics=("arbitrary",),
                                             vmem_limit_bytes=VMEM_LIMIT_BYTES),
        name="mod",
    )(c_all, w_ada.astype(BF16), b_ada.reshape(1, dn))


def _pre_norm(x_ref, mod_ref, gmix_ref, r0, nr):
    nb, _, d = x_ref.shape
    mod = mod_ref[...]
    h = _rms(x_ref[:, r0:r0 + nr, :], gmix_ref[...]) * (1.0 + mod[:, 1:2, :]) + mod[:, 0:1, :]
    return h.reshape(nb * nr, d).astype(BF16)


def _pre_mid(proj, r0, nr, kvg_ref, qg_ref, convw_ref, convb_ref, cos128_ref, sin128_ref,
             yconv_ref, lat_ref, krope_ref, carry_ref):
    nb = lat_ref.shape[0]
    latent = _rms(proj[:, C_KV:C_KV + KV_LORA], kvg_ref[...])
    lat_ref[:, r0:r0 + nr, :] = latent.reshape(nb, nr, KV_LORA)

    g1 = proj[:, C_KR:C_KR + HEAD_SLAB].reshape(nb, nr, HEAD_SLAB)
    g2 = proj[:, C_KRS:C_KRS + HEAD_SLAB].reshape(nb, nr, HEAD_SLAB)
    kr128 = g1 * cos128_ref[r0:r0 + nr, :][None] + g2 * sin128_ref[r0:r0 + nr, :][None]
    krope_ref[:, r0:r0 + nr, :] = kr128[:, :, :QK_ROPE]

    u = proj[:, C_U:C_U + D_CONV]
    g_b = proj[:, C_GB:C_GB + D_CONV].reshape(nb, nr, D_CONV)
    g_c = proj[:, C_GC:C_GC + D_CONV]
    gated = (g_c * u).reshape(nb, nr, D_CONV)
    carry_ref[:, 8 + r0:8 + r0 + nr, :] = gated
    prev2 = carry_ref[:, 6 + r0:6 + r0 + nr, :]
    prev1 = carry_ref[:, 7 + r0:7 + r0 + nr, :]
    cw = convw_ref[...]
    conv = cw[0:1][None] * prev2 + cw[1:2][None] * prev1 + cw[2:3][None] * gated + convb_ref[...][None]
    yconv_ref[:, r0:r0 + nr, :] = (g_b * conv).astype(BF16)

    cqn = _rms(proj[:, C_Q:C_Q + Q_LORA], qg_ref[...]).astype(BF16)
    return latent.astype(BF16), kr128, cqn, gated[:, nr - 2:nr, :]


def _pre_prompt_kernel(x_ref, mod_ref, cprev_ref, gmix_ref, w_in_ref, kvg_ref, convw_ref, convb_ref,
                       cos128_ref, sin128_ref, qg_ref, w_uqT_ref, w_uk_ref, w_uvT_ref, cosT_ref, sinT_ref,
                       yconv_ref, lat_ref, krope_ref, cstate_ref, qT_ref, k_ref, vT_ref,
                       carry_ref, *, tk, n_chunks):
    _, ts, _ = x_ref.shape
    nr = ts // n_chunks

    @pl.when(pl.program_id(1) == 0)
    def _():
        carry_ref[:, 6:8, :] = cprev_ref[...]

    hs = [_pre_norm(x_ref, mod_ref, gmix_ref, c * nr, nr) for c in range(n_chunks)]
    projs = [_dot(h, w_in_ref[...]) for h in hs]

    ups = []
    for c in range(n_chunks):
        lat_bf, kr128, cqn, state = _pre_mid(
            projs[c], c * nr, nr, kvg_ref, qg_ref, convw_ref, convb_ref, cos128_ref, sin128_ref,
            yconv_ref, lat_ref, krope_ref, carry_ref)
        qT = _dot_nt(w_uqT_ref[...], cqn) * (SM_SCALE * LOG2E)
        k_nope = _dot(lat_bf, w_uk_ref[...])
        vT = _dot_nt(w_uvT_ref[...], lat_bf)
        ups.append((qT, k_nope, vT, kr128.reshape(nr, HEAD_SLAB)))
    cstate_ref[...] = state
    carry_ref[:, 6:8, :] = state

    step = min(nr, tk)
    row = lax.broadcasted_iota(jnp.int32, (V_ROWS - V_HEAD, step), 0)
    ones_rows = jnp.where(row == 0, 1.0, 0.0).astype(BF16)
    for c, (qT, k_nope, vT, kr128) in enumerate(ups):
        r0 = c * nr
        cs = cosT_ref[:, r0:r0 + nr]
        sn = sinT_ref[:, r0:r0 + nr]
        for h in range(N_HEADS):
            b0 = h * HEAD_SLAB
            x1 = qT[b0:b0 + HALF]
            x2 = qT[b0 + HALF:b0 + QK_ROPE]
            qT_ref[0, b0:b0 + HALF, r0:r0 + nr] = (x1 * cs - x2 * sn).astype(BF16)
            qT_ref[0, b0 + HALF:b0 + QK_ROPE, r0:r0 + nr] = (x1 * sn + x2 * cs).astype(BF16)
            qT_ref[0, b0 + QK_ROPE:b0 + HEAD_SLAB, r0:r0 + nr] = qT[b0 + QK_ROPE:b0 + HEAD_SLAB].astype(BF16)
            k_ref[0, r0:r0 + nr, b0:b0 + HEAD_SLAB] = (k_nope[:, b0:b0 + HEAD_SLAB] + kr128).astype(BF16)
            for off in range(0, nr, step):
                cb, lo = divmod(r0 + off, tk)
                vT_ref[0, h, cb, 0:V_HEAD, lo:lo + step] = (
                    vT[h * V_HEAD:(h + 1) * V_HEAD, off:off + step].astype(BF16))
                vT_ref[0, h, cb, V_HEAD:V_ROWS, lo:lo + step] = ones_rows


def _pre_sample_kernel(x_ref, mod_ref, cprev_ref, gmix_ref, w_in_ref, kvg_ref, convw_ref, convb_ref,
                       cos128_ref, sin128_ref, qg_ref, w_uq_ref, w_uqs_ref, cosf_ref, sinf_ref,
                       yconv_ref, lat_ref, krope_ref, cstate_ref, q_ref, carry_ref):
    nb, ts, _ = x_ref.shape
    carry_ref[:, 6:8, :] = cprev_ref[...]
    proj = _dot(_pre_norm(x_ref, mod_ref, gmix_ref, 0, ts), w_in_ref[...])
    _, _, cqn, state = _pre_mid(
        proj, 0, ts, kvg_ref, qg_ref, convw_ref, convb_ref, cos128_ref, sin128_ref,
        yconv_ref, lat_ref, krope_ref, carry_ref)
    cstate_ref[...] = state
    q = (_dot(cqn, w_uq_ref[...]) * SM_SCALE).reshape(nb, ts, D_SLAB)
    qs = (_dot(cqn, w_uqs_ref[...]) * SM_SCALE).reshape(nb, ts, D_SLAB)
    q_ref[...] = (q * cosf_ref[...][None] + qs * sinf_ref[...][None]).astype(BF16)


def _pre_call(x, mod, cprev, w, tabs, *, nb, ts, tk, prompt, n_chunks=1):
    b, s, d = x.shape
    grid = (b // nb, s // ts)
    tok = lambda width: pl.BlockSpec((nb, ts, width), lambda i, j: (i, j, 0))
    per_b = lambda rows, width: pl.BlockSpec((nb, rows, width), lambda i, j: (i, 0, 0))
    in_specs = [tok(d), per_b(6, d), per_b(CONV_W - 1, D_CONV),
                _const_spec((1, d)), _const_spec((d, D_PROJ)), _const_spec((1, KV_LORA)),
                _const_spec((CONV_W, D_CONV)), _const_spec((1, D_CONV)),
                pl.BlockSpec((ts, HEAD_SLAB), lambda i, j: (j, 0)),
                pl.BlockSpec((ts, HEAD_SLAB), lambda i, j: (j, 0)),
                _const_spec((1, Q_LORA))]
    args = [x, mod, cprev, w["g_mix"], w["w_in"], w["kv_g"], w["conv_w"], w["conv_b"],
            tabs["cos128"], tabs["sin128"], w["q_g"]]
    out_specs = [tok(D_CONV), tok(KV_LORA), tok(QK_ROPE), per_b(CONV_W - 1, D_CONV)]
    out_shape = [jax.ShapeDtypeStruct((b, s, D_CONV), BF16),
                 jax.ShapeDtypeStruct((b, s, KV_LORA), F32),
                 jax.ShapeDtypeStruct((b, s, QK_ROPE), F32),
                 jax.ShapeDtypeStruct((b, CONV_W - 1, D_CONV), F32)]
    if prompt:
        nr = ts // n_chunks
        assert nb == 1 and ts % tk == 0 and ts % n_chunks == 0 and nr % HEAD_SLAB == 0
        assert tk % nr == 0 or nr % tk == 0
        kernel = functools.partial(_pre_prompt_kernel, tk=tk, n_chunks=n_chunks)
        in_specs += [_const_spec((D_SLAB, Q_LORA)), _const_spec((KV_LORA, D_SLAB)),
                     _const_spec((D_ATTN, KV_LORA)),
                     pl.BlockSpec((HALF, ts), lambda i, j: (0, j)),
                     pl.BlockSpec((HALF, ts), lambda i, j: (0, j))]
        args += [w["w_uqT"], w["w_uk"], w["w_uvT"], tabs["cosT"], tabs["sinT"]]
        out_specs += [pl.BlockSpec((1, D_SLAB, ts), lambda i, j: (i, 0, j)),
                      tok(D_SLAB),
                      pl.BlockSpec((1, N_HEADS, ts // tk, V_ROWS, tk), lambda i, j: (i, 0, j, 0, 0))]
        out_shape += [jax.ShapeDtypeStruct((b, D_SLAB, s), BF16),
                      jax.ShapeDtypeStruct((b, s, D_SLAB), BF16),
                      jax.ShapeDtypeStruct((b, N_HEADS, s // tk, V_ROWS, tk), BF16)]
    else:
        assert ts == s
        kernel = _pre_sample_kernel
        in_specs += [_const_spec((Q_LORA, D_SLAB)), _const_spec((Q_LORA, D_SLAB)),
                     _const_spec((ts, D_SLAB)), _const_spec((ts, D_SLAB))]
        args += [w["w_uq"], w["w_uqs"], tabs["cosf"], tabs["sinf"]]
        out_specs += [tok(D_SLAB)]
        out_shape += [jax.ShapeDtypeStruct((b, s, D_SLAB), BF16)]
    return pl.pallas_call(
        kernel, grid=grid, in_specs=in_specs, out_specs=out_specs, out_shape=out_shape,
        scratch_shapes=[pltpu.VMEM((nb, ts + 8, D_CONV), F32)],
        compiler_params=pltpu.CompilerParams(dimension_semantics=("arbitrary", "arbitrary"),
                                             vmem_limit_bytes=VMEM_LIMIT_BYTES),
        name="pre_prompt" if prompt else "pre_sample",
    )(*args)


def _attn_kernel(qT_ref, k_ref, vT_ref, o_ref, s0_ref, s1_ref, mb0_ref, mb1_ref, acc_ref, m_ref,
                 *, tq, tk, heads):
    i = pl.program_id(2)
    m_ref[...] = jnp.full(m_ref.shape, NEG, F32)
    acc_ref[...] = jnp.zeros(acc_ref.shape, F32)
    even = (s0_ref, mb0_ref)
    odd = (s1_ref, mb1_ref)

    def score_head(j, dst, hh, masked, col0=0):
        s_ref, mb_ref = dst
        k0 = pl.multiple_of(j * tk, tk)
        kb = k_ref[0, pl.ds(k0, tk), hh * HEAD_SLAB:(hh + 1) * HEAD_SLAB]
        qT = qT_ref[0, hh * HEAD_SLAB:(hh + 1) * HEAD_SLAB, col0:tq]
        sT = _dot(kb, qT)
        if masked:
            kpos = k0 + lax.broadcasted_iota(jnp.int32, (tk, 1), 0)
            qpos = i * tq + col0 + lax.broadcasted_iota(jnp.int32, (1, tq - col0), 1)
            sT = jnp.where(kpos <= (qpos | (CHUNK - 1)), sT, NEG)
        s_ref[hh, :, col0:tq] = sT
        mb_ref[hh, :, col0:tq] = jnp.max(sT, axis=0, keepdims=True)

    def fold_head(j, src, hh, col0=0):
        s_ref, mb_ref = src
        m_old = m_ref[hh, :, col0:tq]
        m_new = jnp.maximum(m_old, mb_ref[hh, :, col0:tq])
        alpha = jnp.exp2(m_old - m_new)
        pT = jnp.exp2(s_ref[hh, :, col0:tq] - m_new).astype(BF16)
        pv = _dot(vT_ref[0, hh, j], pT)
        acc_ref[hh, :, col0:tq] = acc_ref[hh, :, col0:tq] * alpha + pv
        m_ref[hh, :, col0:tq] = m_new

    def stage(j, src, dst, masked, col0=0):
        for hh in range(heads):
            score_head(j + 1, dst, hh, masked, col0)
            fold_head(j, src, hh)

    jd = 2 * i

    @pl.when(i > 0)
    def _():
        for hh in range(heads):
            score_head(0, even, hh, False)

    @pl.when(i == 0)
    def _():
        for hh in range(heads):
            score_head(0, even, hh, True)

    def body(jj, carry):
        j = 2 * jj
        stage(j, even, odd, False)
        stage(j + 1, odd, even, False)
        return carry

    lax.fori_loop(0, i - 1, body, 0)

    @pl.when(i > 0)
    def _():
        stage(jd - 2, even, odd, False)
        stage(jd - 1, odd, even, True)

    stage(jd, even, odd, True, tk)
    for hh in range(heads):
        fold_head(jd + 1, odd, hh, tk)

    outs = []
    for hh in range(heads):
        acc = acc_ref[hh]
        outs.append(acc[0:V_HEAD] / acc[V_HEAD:V_HEAD + 1])
    oT = jnp.concatenate(outs, axis=0)
    o_ref[0] = oT.T.astype(BF16)


def _attn_call(qT, k, vT, *, tq, tk, heads=N_HEADS):
    b, _, s = qT.shape
    nkb = s // tk
    assert tk % CHUNK == 0 and tq == 2 * tk and s % tq == 0 and N_HEADS % heads == 0
    grid = (b, N_HEADS // heads, s // tq)
    once = pl.Buffered(1)
    return pl.pallas_call(
        functools.partial(_attn_kernel, tq=tq, tk=tk, heads=heads),
        grid=grid,
        in_specs=[pl.BlockSpec((1, heads * HEAD_SLAB, tq), lambda bi, h, i: (bi, h, i)),
                  pl.BlockSpec((1, s, heads * HEAD_SLAB), lambda bi, h, i: (bi, 0, h), pipeline_mode=once),
                  pl.BlockSpec((1, heads, nkb, V_ROWS, tk), lambda bi, h, i: (bi, h, 0, 0, 0),
                               pipeline_mode=once)],
        out_specs=pl.BlockSpec((1, tq, heads * V_HEAD), lambda bi, h, i: (bi, i, h)),
        out_shape=jax.ShapeDtypeStruct((b, s, D_ATTN), BF16),
        scratch_shapes=[pltpu.VMEM((heads, tk, tq), F32), pltpu.VMEM((heads, tk, tq), F32),
                        pltpu.VMEM((heads, 1, tq), F32), pltpu.VMEM((heads, 1, tq), F32),
                        pltpu.VMEM((heads, V_ROWS, tq), F32), pltpu.VMEM((heads, 1, tq), F32)],
        compiler_params=pltpu.CompilerParams(dimension_semantics=("arbitrary", "arbitrary", "arbitrary"),
                                             vmem_limit_bytes=VMEM_LIMIT_BYTES),
        name="attn_prompt",
    )(qT, k, vT)


def _sattn_kernel(q_ref, latp_ref, krp_ref, latn_ref, krn_ref, w_ukT_ref, w_uv_ref, o_ref, o_scr):
    sd = q_ref.shape[1]
    latp = latp_ref[0].astype(BF16)
    latn = latn_ref[0].astype(BF16)
    krp = krp_ref[0].astype(BF16)
    krn = krn_ref[0].astype(BF16)
    qa, qr = [], []
    for h in range(N_HEADS):
        qh = q_ref[0, :, h * HEAD_SLAB:(h + 1) * HEAD_SLAB]
        qa.append(_dot(qh, w_ukT_ref[h]).astype(BF16))
        qr.append(qh[:, 0:QK_ROPE])
    qa = jnp.concatenate(qa, axis=0)
    qr = jnp.concatenate(qr, axis=0)
    s_p = _dot_nt(qa, latp) + _dot_nt(qr, krp)
    s_n = _dot_nt(qa, latn) + _dot_nt(qr, krn)
    m = jnp.maximum(jnp.max(s_p, axis=-1, keepdims=True), jnp.max(s_n, axis=-1, keepdims=True))
    p_p = jnp.exp(s_p - m)
    p_n = jnp.exp(s_n - m)
    l = jnp.sum(p_p, axis=-1, keepdims=True) + jnp.sum(p_n, axis=-1, keepdims=True)
    o_lat = (_dot(p_p.astype(BF16), latp) + _dot(p_n.astype(BF16), latn)) / l
    o_all = _dot(o_lat.astype(BF16), w_uv_ref[...])
    for h in range(N_HEADS):
        o_scr[:, h * V_HEAD:(h + 1) * V_HEAD] = o_all[h * sd:(h + 1) * sd, h * V_HEAD:(h + 1) * V_HEAD]
    o_ref[0] = o_scr[...].astype(BF16)


def _sattn_call(q, lat_past, kr_past, lat_new, kr_new, w_ukT, w_uv):
    b, sd, _ = q.shape
    p = lat_past.shape[1]
    assert (p + sd - 1) // CHUNK <= p // CHUNK
    return pl.pallas_call(
        _sattn_kernel,
        grid=(b,),
        in_specs=[pl.BlockSpec((1, sd, D_SLAB), lambda i: (i, 0, 0)),
                  pl.BlockSpec((1, p, KV_LORA), lambda i: (i, 0, 0)),
                  pl.BlockSpec((1, p, QK_ROPE), lambda i: (i, 0, 0)),
                  pl.BlockSpec((1, sd, KV_LORA), lambda i: (i, 0, 0)),
                  pl.BlockSpec((1, sd, QK_ROPE), lambda i: (i, 0, 0)),
                  _const_spec((N_HEADS, HEAD_SLAB, KV_LORA)), _const_spec((KV_LORA, D_ATTN))],
        out_specs=pl.BlockSpec((1, sd, D_ATTN), lambda i: (i, 0, 0)),
        out_shape=jax.ShapeDtypeStruct((b, sd, D_ATTN), BF16),
        scratch_shapes=[pltpu.VMEM((sd, D_ATTN), F32)],
        compiler_params=pltpu.CompilerParams(dimension_semantics=("arbitrary",),
                                             vmem_limit_bytes=VMEM_LIMIT_BYTES),
        name="attn_sample",
    )(q, lat_past, kr_past, lat_new, kr_new, w_ukT, w_uv)


def _post_kernel(x_ref, attn_ref, yconv_ref, mod_ref, w_oa_ref, w_oc_ref, gffn_ref,
                 w_gate_ref, w_up_ref, w_down_ref, gfin_ref, y_ref, *, n_chunks):
    nb, ts, d = x_ref.shape
    nr = ts // n_chunks
    rows = nb * nr
    mod = mod_ref[...]
    spans = [slice(c * nr, (c + 1) * nr) for c in range(n_chunks)]

    mixes = []
    for sp in spans:
        a = attn_ref[:, sp, :].reshape(rows, D_ATTN)
        yc = yconv_ref[:, sp, :].reshape(rows, D_CONV)
        mixes.append(_dot(a, w_oa_ref[...]) + _dot(yc, w_oc_ref[...]))
    x1s, gus = [], []
    for sp, mix in zip(spans, mixes):
        x1 = x_ref[:, sp, :] + mod[:, 2:3, :] * mix.reshape(nb, nr, d)
        h = _rms(x1, gffn_ref[...]) * (1.0 + mod[:, 4:5, :]) + mod[:, 3:4, :]
        hb = h.reshape(rows, d).astype(BF16)
        x1s.append(x1)
        gus.append((_dot(hb, w_gate_ref[...]), _dot(hb, w_up_ref[...])))
    ffs = []
    for gate, up in gus:
        act = ((gate * jax.nn.sigmoid(gate)) * up).astype(BF16)
        ffs.append(_dot(act, w_down_ref[...]))
    for sp, x1, ff in zip(spans, x1s, ffs):
        x2 = x1 + mod[:, 5:6, :] * ff.reshape(nb, nr, d)
        y_ref[:, sp, :] = _rms(x2, gfin_ref[...])


def _post_call(x, attn, yconv, mod, w, *, nb, ts, n_chunks=1):
    b, s, d = x.shape
    dff = w["w_gate"].shape[1]
    assert ts % n_chunks == 0 and (ts // n_chunks) % 16 == 0
    tok = lambda width: pl.BlockSpec((nb, ts, width), lambda i, j: (i, j, 0))
    return pl.pallas_call(
        functools.partial(_post_kernel, n_chunks=n_chunks),
        grid=(b // nb, s // ts),
        in_specs=[tok(d), tok(D_ATTN), tok(D_CONV),
                  pl.BlockSpec((nb, 6, d), lambda i, j: (i, 0, 0)),
                  _const_spec((D_ATTN, d)), _const_spec((D_CONV, d)), _const_spec((1, d)),
                  _const_spec((d, dff)), _const_spec((d, dff)), _const_spec((dff, d)),
                  _const_spec((1, d))],
        out_specs=tok(d),
        out_shape=jax.ShapeDtypeStruct((b, s, d), F32),
        compiler_params=pltpu.CompilerParams(dimension_semantics=("arbitrary", "arbitrary"),
                                             vmem_limit_bytes=VMEM_LIMIT_BYTES),
        name="post",
    )(x, attn, yconv, mod, w["w_oa"], w["w_oc"], w["g_ffn"], w["w_gate"], w["w_up"], w["w_down"], w["g_fin"])


def _prep_weights(w_in, w_uq, w_ukv, w_out, norm_mix_g, q_norm_g, kv_norm_g, conv_w, conv_b,
                  norm_ffn_g, w_gate, w_up, w_down, final_norm_g):
    d = w_in.shape[0]
    i1 = Q_LORA
    i2 = i1 + KV_LORA
    i3 = i2 + QK_ROPE
    zpad = jnp.zeros((d, HEAD_SLAB - QK_ROPE), w_in.dtype)
    k_r = w_in[:, i2:i3]
    k_r_sw = jnp.concatenate([k_r[:, HALF:], k_r[:, :HALF]], axis=1)
    w_in_p = jnp.concatenate([w_in[:, :i2], k_r, zpad, k_r_sw, zpad, w_in[:, i3:]], axis=1)
    assert w_in_p.shape[1] == D_PROJ

    wq = w_uq.reshape(Q_LORA, N_HEADS, QK_NOPE + QK_ROPE)
    zq = jnp.zeros((Q_LORA, N_HEADS, HEAD_SLAB - QK_NOPE - QK_ROPE), w_uq.dtype)
    w_uq_p = jnp.concatenate([wq[..., QK_NOPE:], wq[..., :QK_NOPE], zq], axis=-1).reshape(Q_LORA, D_SLAB)
    zq2 = jnp.zeros((Q_LORA, N_HEADS, HEAD_SLAB - QK_ROPE), w_uq.dtype)
    w_uq_s = jnp.concatenate([wq[..., QK_NOPE + HALF:], wq[..., QK_NOPE:QK_NOPE + HALF], zq2],
                             axis=-1).reshape(Q_LORA, D_SLAB)

    wkv = w_ukv.reshape(KV_LORA, N_HEADS, QK_NOPE + V_HEAD)
    zk1 = jnp.zeros((KV_LORA, N_HEADS, QK_ROPE), w_ukv.dtype)
    zk2 = jnp.zeros((KV_LORA, N_HEADS, HEAD_SLAB - QK_ROPE - QK_NOPE), w_ukv.dtype)
    w_uk_p = jnp.concatenate([zk1, wkv[..., :QK_NOPE], zk2], axis=-1).reshape(KV_LORA, D_SLAB)
    w_uv = wkv[..., QK_NOPE:].reshape(KV_LORA, D_ATTN)
    w_ukT = w_uk_p.T.reshape(N_HEADS, HEAD_SLAB, KV_LORA)

    return {
        "g_mix": norm_mix_g.reshape(1, -1), "w_in": w_in_p.astype(BF16),
        "q_g": q_norm_g.reshape(1, -1), "kv_g": kv_norm_g.reshape(1, -1),
        "conv_w": conv_w, "conv_b": conv_b.reshape(1, -1),
        "w_uq": w_uq_p.astype(BF16), "w_uqs": w_uq_s.astype(BF16), "w_uqT": w_uq_p.T.astype(BF16),
        "w_uk": w_uk_p.astype(BF16), "w_ukT": w_ukT.astype(BF16),
        "w_uv": w_uv.astype(BF16), "w_uvT": w_uv.T.astype(BF16),
        "w_oa": w_out[:D_ATTN].astype(BF16), "w_oc": w_out[D_ATTN:].astype(BF16),
        "g_ffn": norm_ffn_g.reshape(1, -1),
        "w_gate": w_gate.astype(BF16), "w_up": w_up.astype(BF16), "w_down": w_down.astype(BF16),
        "g_fin": final_norm_g.reshape(1, -1),
    }


def _rope_tables(pos):
    n = pos.shape[0]
    inv_freq = ROPE_THETA ** (-jnp.arange(HALF, dtype=F32) / HALF)
    ang = pos.astype(F32)[:, None] * inv_freq[None, :]
    cos = jnp.cos(ang)
    sin = jnp.sin(ang)
    z96 = jnp.zeros((n, HEAD_SLAB - QK_ROPE), F32)
    cos128 = jnp.concatenate([cos, cos, z96], axis=1)
    sin128 = jnp.concatenate([-sin, sin, z96], axis=1)
    ones96 = jnp.ones((n, HEAD_SLAB - QK_ROPE), F32)
    cosf = jnp.tile(jnp.concatenate([cos, cos, ones96], axis=1), (1, N_HEADS))
    sinf = jnp.tile(sin128, (1, N_HEADS))
    return {"cos128": cos128, "sin128": sin128, "cosT": cos.T, "sinT": sin.T, "cosf": cosf, "sinf": sinf}


def kernel(x_prompt, x_sample, c_prompt, c_sample, cache_kv_latent, cache_k_rope, state_conv,
           w_ada, b_ada, norm_mix_g, w_in, q_norm_g, w_uq, kv_norm_g, w_ukv,
           conv_w, conv_b, w_out, norm_ffn_g, w_gate, w_up, w_down, final_norm_g):
    depth = w_in.shape[0]
    assert depth == 1
    bp, s, d = x_prompt.shape
    bs, sd, _ = x_sample.shape
    p = cache_kv_latent.shape[2]
    l = 0
    w = _prep_weights(w_in[l], w_uq[l], w_ukv[l], w_out[l], norm_mix_g[l], q_norm_g[l], kv_norm_g[l],
                      conv_w[l], conv_b[l], norm_ffn_g[l], w_gate[l], w_up[l], w_down[l], final_norm_g)
    tabs_p = _rope_tables(jnp.arange(s, dtype=jnp.int32))
    tabs_s = _rope_tables(p + jnp.arange(sd, dtype=jnp.int32))

    mod = _mod_call(jnp.concatenate([c_prompt, c_sample], axis=0), w_ada[l], b_ada[l])
    mod = mod.reshape(bp + bs, 6, d)
    mod_p, mod_s = mod[:bp], mod[bp:]

    tq, tk = 512, 256
    conv_zero = jnp.zeros((bp, CONV_W - 1, D_CONV), x_prompt.dtype)
    yconv_p, lat_p, kr_p, cv_p, qT, k_all, vT = _pre_call(
        x_prompt, mod_p, conv_zero, w, tabs_p, nb=1, ts=1024, tk=tk, prompt=True, n_chunks=4)
    attn_p = _attn_call(qT, k_all, vT, tq=tq, tk=tk)
    y_prompt = _post_call(x_prompt, attn_p, yconv_p, mod_p, w, nb=1, ts=1024, n_chunks=4)

    nb_s = 8
    yconv_s, lat_s, kr_s, cv_s, q_s = _pre_call(
        x_sample, mod_s, state_conv[l], w, tabs_s, nb=nb_s, ts=sd, tk=tk, prompt=False)
    attn_s = _sattn_call(q_s, cache_kv_latent[l], cache_k_rope[l], lat_s, kr_s, w["w_ukT"], w["w_uv"])
    y_sample = _post_call(x_sample, attn_s, yconv_s, mod_s, w, nb=nb_s, ts=sd)

    return (y_prompt, y_sample, lat_p[None], kr_p[None], cv_p[None], lat_s[None], kr_s[None], cv_s[None])
```

```python
import functools

import jax
import jax.numpy as jnp
from jax import lax
from jax.experimental import pallas as pl
from jax.experimental.pallas import tpu as pltpu

CHUNK = 64
N_HEADS = 8
QK_NOPE = 64
QK_ROPE = 32
V_HEAD = 64
KV_LORA = 256
Q_LORA = 768
D_CONV = 512
CONV_W = 3
ROPE_THETA = 10000.0
EPS = 1e-6
SM_SCALE = (QK_NOPE + QK_ROPE) ** -0.5
LOG2E = 1.4426950408889634

HALF = QK_ROPE // 2
HEAD_SLAB = 128
D_SLAB = N_HEADS * HEAD_SLAB
D_ATTN = N_HEADS * V_HEAD
V_ROWS = V_HEAD + 16

C_Q = 0
C_KV = C_Q + Q_LORA
C_KR = C_KV + KV_LORA
C_KRS = C_KR + HEAD_SLAB
C_U = C_KRS + HEAD_SLAB
C_GB = C_U + D_CONV
C_GC = C_GB + D_CONV
D_PROJ = C_GC + D_CONV

VMEM_LIMIT_BYTES = 56 * 1024 * 1024

BF16 = jnp.bfloat16
F32 = jnp.float32
NEG = float(jnp.finfo(jnp.float32).min)


def _rms(x, g):
    ms = jnp.mean(x * x, axis=-1, keepdims=True)
    return x * lax.rsqrt(ms + EPS) * g


def _dot(a, b):
    return jnp.dot(a, b, preferred_element_type=F32)


def _dot_nt(a, b):
    return lax.dot_general(a, b, (((1,), (1,)), ((), ())), preferred_element_type=F32)


def _const_spec(shape):
    nd = len(shape)
    return pl.BlockSpec(shape, lambda *_: (0,) * nd, pipeline_mode=pl.Buffered(1))


def _mod_kernel(c_ref, w_ref, b_ref, o_ref):
    c = c_ref[...]
    a = (c * jax.nn.sigmoid(c)).astype(BF16)
    o_ref[...] = _dot(a, w_ref[...].astype(BF16)) + b_ref[...]


def _mod_call(c_all, w_ada, b_ada):
    n, d = c_all.shape
    dn = w_ada.shape[1]
    tn = dn // 4
    return pl.pallas_call(
        _mod_kernel,
        grid=(dn // tn,),
        in_specs=[pl.BlockSpec((n, d), lambda j: (0, 0)),
                  pl.BlockSpec((d, tn), lambda j: (0, j)),
                  pl.BlockSpec((1, tn), lambda j: (0, j))],
        out_specs=pl.BlockSpec((n, tn), lambda j: (0, j)),
        out_shape=jax.ShapeDtypeStruct((n, dn), F32),
        compiler_params=pltpu.CompilerParams(dimension_semantics=("arbitrary",),
                                             vmem_limit_bytes=VMEM_LIMIT_BYTES),
        name="mod",
    )(c_all, w_ada, b_ada.reshape(1, dn))


def _pre_norm(x_ref, mod_ref, gmix_ref, r0, nr):
    nb, _, d = x_ref.shape
    mod = mod_ref[...]
    h = _rms(x_ref[:, r0:r0 + nr, :], gmix_ref[...]) * (1.0 + mod[:, 1:2, :]) + mod[:, 0:1, :]
    return h.reshape(nb * nr, d).astype(BF16)


def _pre_mid(proj, r0, nr, kvg_ref, qg_ref, convw_ref, convb_ref, cos128_ref, sin128_ref,
             yconv_ref, lat_ref, krope_ref, carry_ref):
    nb = lat_ref.shape[0]
    latent = _rms(proj[:, C_KV:C_KV + KV_LORA], kvg_ref[...])
    lat_ref[:, r0:r0 + nr, :] = latent.reshape(nb, nr, KV_LORA)

    g1 = proj[:, C_KR:C_KR + HEAD_SLAB].reshape(nb, nr, HEAD_SLAB)
    g2 = proj[:, C_KRS:C_KRS + HEAD_SLAB].reshape(nb, nr, HEAD_SLAB)
    kr128 = g1 * cos128_ref[r0:r0 + nr, :][None] + g2 * sin128_ref[r0:r0 + nr, :][None]
    krope_ref[:, r0:r0 + nr, :] = kr128[:, :, :QK_ROPE]

    u = proj[:, C_U:C_U + D_CONV]
    g_b = proj[:, C_GB:C_GB + D_CONV].reshape(nb, nr, D_CONV)
    g_c = proj[:, C_GC:C_GC + D_CONV]
    gated = (g_c * u).reshape(nb, nr, D_CONV)
    carry_ref[:, 8 + r0:8 + r0 + nr, :] = gated
    prev2 = carry_ref[:, 6 + r0:6 + r0 + nr, :]
    prev1 = carry_ref[:, 7 + r0:7 + r0 + nr, :]
    cw = convw_ref[...]
    conv = cw[0:1][None] * prev2 + cw[1:2][None] * prev1 + cw[2:3][None] * gated + convb_ref[...][None]
    yconv_ref[:, r0:r0 + nr, :] = (g_b * conv).astype(BF16)

    cqn = _rms(proj[:, C_Q:C_Q + Q_LORA], qg_ref[...]).astype(BF16)
    return latent.astype(BF16), kr128, cqn, gated[:, nr - 2:nr, :]


def _pre_prompt_kernel(x_ref, mod_ref, cprev_ref, gmix_ref, w_in_ref, kvg_ref, convw_ref, convb_ref,
                       cos128_ref, sin128_ref, qg_ref, w_uqT_ref, w_uk_ref, w_uvT_ref, cosT_ref, sinT_ref,
                       yconv_ref, lat_ref, krope_ref, cstate_ref, qT_ref, k_ref, vT_ref,
                       carry_ref, *, tk, n_chunks):
    _, ts, _ = x_ref.shape
    nr = ts // n_chunks

    @pl.when(pl.program_id(1) == 0)
    def _():
        carry_ref[:, 6:8, :] = cprev_ref[...]

    hs = [_pre_norm(x_ref, mod_ref, gmix_ref, c * nr, nr) for c in range(n_chunks)]
    projs = [_dot(h, w_in_ref[...]) for h in hs]

    ups = []
    for c in range(n_chunks):
        lat_bf, kr128, cqn, state = _pre_mid(
            projs[c], c * nr, nr, kvg_ref, qg_ref, convw_ref, convb_ref, cos128_ref, sin128_ref,
            yconv_ref, lat_ref, krope_ref, carry_ref)
        qT = _dot_nt(w_uqT_ref[...], cqn) * (SM_SCALE * LOG2E)
        k_nope = _dot(lat_bf, w_uk_ref[...])
        vT = _dot_nt(w_uvT_ref[...], lat_bf)
        ups.append((qT, k_nope, vT, kr128.reshape(nr, HEAD_SLAB)))
    cstate_ref[...] = state
    carry_ref[:, 6:8, :] = state

    step = min(nr, tk)
    row = lax.broadcasted_iota(jnp.int32, (V_ROWS - V_HEAD, step), 0)
    ones_rows = jnp.where(row == 0, 1.0, 0.0).astype(BF16)
    for c, (qT, k_nope, vT, kr128) in enumerate(ups):
        r0 = c * nr
        cs = cosT_ref[:, r0:r0 + nr]
        sn = sinT_ref[:, r0:r0 + nr]
        for h in range(N_HEADS):
            b0 = h * HEAD_SLAB
            x1 = qT[b0:b0 + HALF]
            x2 = qT[b0 + HALF:b0 + QK_ROPE]
            qT_ref[0, b0:b0 + HALF, r0:r0 + nr] = (x1 * cs - x2 * sn).astype(BF16)
            qT_ref[0, b0 + HALF:b0 + QK_ROPE, r0:r0 + nr] = (x1 * sn + x2 * cs).astype(BF16)
            qT_ref[0, b0 + QK_ROPE:b0 + HEAD_SLAB, r0:r0 + nr] = qT[b0 + QK_ROPE:b0 + HEAD_SLAB].astype(BF16)
            k_ref[0, r0:r0 + nr, b0:b0 + HEAD_SLAB] = (k_nope[:, b0:b0 + HEAD_SLAB] + kr128).astype(BF16)
            for off in range(0, nr, step):
                cb, lo = divmod(r0 + off, tk)
                vT_ref[0, h, cb, 0:V_HEAD, lo:lo + step] = (
                    vT[h * V_HEAD:(h + 1) * V_HEAD, off:off + step].astype(BF16))
                vT_ref[0, h, cb, V_HEAD:V_ROWS, lo:lo + step] = ones_rows


def _pre_sample_kernel(x_ref, mod_ref, cprev_ref, gmix_ref, w_in_ref, kvg_ref, convw_ref, convb_ref,
                       cos128_ref, sin128_ref, qg_ref, w_uq_ref, w_uqs_ref, cosf_ref, sinf_ref,
                       yconv_ref, lat_ref, krope_ref, cstate_ref, q_ref, carry_ref):
    nb, ts, _ = x_ref.shape
    carry_ref[:, 6:8, :] = cprev_ref[...]
    proj = _dot(_pre_norm(x_ref, mod_ref, gmix_ref, 0, ts), w_in_ref[...])
    _, _, cqn, state = _pre_mid(
        proj, 0, ts, kvg_ref, qg_ref, convw_ref, convb_ref, cos128_ref, sin128_ref,
        yconv_ref, lat_ref, krope_ref, carry_ref)
    cstate_ref[...] = state
    q = (_dot(cqn, w_uq_ref[...]) * SM_SCALE).reshape(nb, ts, D_SLAB)
    qs = (_dot(cqn, w_uqs_ref[...]) * SM_SCALE).reshape(nb, ts, D_SLAB)
    q_ref[...] = (q * cosf_ref[...][None] + qs * sinf_ref[...][None]).astype(BF16)


def _pre_call(x, mod, cprev, w, tabs, *, nb, ts, tk, prompt, n_chunks=1):
    b, s, d = x.shape
    grid = (b // nb, s // ts)
    tok = lambda width: pl.BlockSpec((nb, ts, width), lambda i, j: (i, j, 0))
    per_b = lambda rows, width: pl.BlockSpec((nb, rows, width), lambda i, j: (i, 0, 0))
    in_specs = [tok(d), per_b(6, d), per_b(CONV_W - 1, D_CONV),
                _const_spec((1, d)), _const_spec((d, D_PROJ)), _const_spec((1, KV_LORA)),
                _const_spec((CONV_W, D_CONV)), _const_spec((1, D_CONV)),
                pl.BlockSpec((ts, HEAD_SLAB), lambda i, j: (j, 0)),
                pl.BlockSpec((ts, HEAD_SLAB), lambda i, j: (j, 0)),
                _const_spec((1, Q_LORA))]
    args = [x, mod, cprev, w["g_mix"], w["w_in"], w["kv_g"], w["conv_w"], w["conv_b"],
            tabs["cos128"], tabs["sin128"], w["q_g"]]
    out_specs = [tok(D_CONV), tok(KV_LORA), tok(QK_ROPE), per_b(CONV_W - 1, D_CONV)]
    out_shape = [jax.ShapeDtypeStruct((b, s, D_CONV), BF16),
                 jax.ShapeDtypeStruct((b, s, KV_LORA), F32),
                 jax.ShapeDtypeStruct((b, s, QK_ROPE), F32),
                 jax.ShapeDtypeStruct((b, CONV_W - 1, D_CONV), F32)]
    if prompt:
        nr = ts // n_chunks
        assert nb == 1 and ts % tk == 0 and ts % n_chunks == 0 and nr % HEAD_SLAB == 0
        assert tk % nr == 0 or nr % tk == 0
        kernel = functools.partial(_pre_prompt_kernel, tk=tk, n_chunks=n_chunks)
        in_specs += [_const_spec((D_SLAB, Q_LORA)), _const_spec((KV_LORA, D_SLAB)),
                     _const_spec((D_ATTN, KV_LORA)),
                     pl.BlockSpec((HALF, ts), lambda i, j: (0, j)),
                     pl.BlockSpec((HALF, ts), lambda i, j: (0, j))]
        args += [w["w_uqT"], w["w_uk"], w["w_uvT"], tabs["cosT"], tabs["sinT"]]
        out_specs += [pl.BlockSpec((1, D_SLAB, ts), lambda i, j: (i, 0, j)),
                      tok(D_SLAB),
                      pl.BlockSpec((1, N_HEADS, ts // tk, V_ROWS, tk), lambda i, j: (i, 0, j, 0, 0))]
        out_shape += [jax.ShapeDtypeStruct((b, D_SLAB, s), BF16),
                      jax.ShapeDtypeStruct((b, s, D_SLAB), BF16),
                      jax.ShapeDtypeStruct((b, N_HEADS, s // tk, V_ROWS, tk), BF16)]
    else:
        assert ts == s
        kernel = _pre_sample_kernel
        in_specs += [_const_spec((Q_LORA, D_SLAB)), _const_spec((Q_LORA, D_SLAB)),
                     _const_spec((ts, D_SLAB)), _const_spec((ts, D_SLAB))]
        args += [w["w_uq"], w["w_uqs"], tabs["cosf"], tabs["sinf"]]
        out_specs += [tok(D_SLAB)]
        out_shape += [jax.ShapeDtypeStruct((b, s, D_SLAB), BF16)]
    return pl.pallas_call(
        kernel, grid=grid, in_specs=in_specs, out_specs=out_specs, out_shape=out_shape,
        scratch_shapes=[pltpu.VMEM((nb, ts + 8, D_CONV), F32)],
        compiler_params=pltpu.CompilerParams(dimension_semantics=("arbitrary", "arbitrary"),
                                             vmem_limit_bytes=VMEM_LIMIT_BYTES),
        name="pre_prompt" if prompt else "pre_sample",
    )(*args)


def _attn_kernel(qT_ref, qT_next_ref, k_ref, vT_ref, o_ref, s0_ref, s1_ref, mb0_ref, mb1_ref, acc_ref, m_ref,
                 *, tq, tk, heads):
    i = pl.program_id(2)
    m_ref[...] = jnp.full(m_ref.shape, NEG, F32)
    acc_ref[...] = jnp.zeros(acc_ref.shape, F32)
    even = (s0_ref, mb0_ref)
    odd = (s1_ref, mb1_ref)

    def score_head(j, dst, hh, masked, col0=0, q_ref=qT_ref):
        s_ref, mb_ref = dst
        k0 = pl.multiple_of(j * tk, tk)
        kb = k_ref[0, pl.ds(k0, tk), hh * HEAD_SLAB:(hh + 1) * HEAD_SLAB]
        qT = q_ref[0, hh * HEAD_SLAB:(hh + 1) * HEAD_SLAB, col0:tq]
        sT = _dot(kb, qT)
        if masked:
            kpos = k0 + lax.broadcasted_iota(jnp.int32, (tk, 1), 0)
            qpos = i * tq + col0 + lax.broadcasted_iota(jnp.int32, (1, tq - col0), 1)
            sT = jnp.where(kpos <= (qpos | (CHUNK - 1)), sT, NEG)
        s_ref[hh, :, col0:tq] = sT
        mb_ref[hh, :, col0:tq] = jnp.max(sT, axis=0, keepdims=True)

    def fold_head(j, src, hh, col0=0):
        s_ref, mb_ref = src
        m_old = m_ref[hh, :, col0:tq]
        m_new = jnp.maximum(m_old, mb_ref[hh, :, col0:tq])
        alpha = jnp.exp2(m_old - m_new)
        pT = jnp.exp2(s_ref[hh, :, col0:tq] - m_new).astype(BF16)
        pv = _dot(vT_ref[0, hh, j], pT)
        acc_ref[hh, :, col0:tq] = acc_ref[hh, :, col0:tq] * alpha + pv
        m_ref[hh, :, col0:tq] = m_new

    def stage(j, src, dst, masked, col0=0):
        for hh in range(heads):
            score_head(j + 1, dst, hh, masked, col0)
            fold_head(j, src, hh)

    jd = 2 * i

    @pl.when(i == 0)
    def _():
        for hh in range(heads):
            score_head(0, even, hh, True)

    def pair(j):
        stage(j, even, odd, False)
        stage(j + 1, odd, even, False)

    def body(jj, carry):
        pair(4 * jj)
        pair(4 * jj + 2)
        return carry

    n_pairs = jnp.maximum(i - 1, 0)
    lax.fori_loop(0, n_pairs >> 1, body, 0)

    @pl.when((n_pairs & 1) == 1)
    def _():
        pair(2 * n_pairs - 2)

    @pl.when(i > 0)
    def _():
        stage(jd - 2, even, odd, False)
        stage(jd - 1, odd, even, True)

    stage(jd, even, odd, True, tk)
    for hh in range(heads):
        score_head(0, even, hh, False, q_ref=qT_next_ref)
        fold_head(jd + 1, odd, hh, tk)

    outs = []
    for hh in range(heads):
        acc = acc_ref[hh]
        outs.append(acc[0:V_HEAD] / acc[V_HEAD:V_HEAD + 1])
    oT = jnp.concatenate(outs, axis=0)
    o_ref[0] = oT.T.astype(BF16)


def _attn_call(qT, k, vT, *, tq, tk, heads=N_HEADS):
    b, _, s = qT.shape
    nkb = s // tk
    assert tk % CHUNK == 0 and tq == 2 * tk and s % tq == 0 and N_HEADS % heads == 0
    n_tiles = s // tq
    grid = (b, N_HEADS // heads, n_tiles)
    once = pl.Buffered(1)
    return pl.pallas_call(
        functools.partial(_attn_kernel, tq=tq, tk=tk, heads=heads),
        grid=grid,
        in_specs=[pl.BlockSpec((1, heads * HEAD_SLAB, tq), lambda bi, h, i: (bi, h, i)),
                  pl.BlockSpec((1, heads * HEAD_SLAB, tq),
                               lambda bi, h, i: (bi, h, jnp.minimum(i + 1, n_tiles - 1))),
                  pl.BlockSpec((1, s, heads * HEAD_SLAB), lambda bi, h, i: (bi, 0, h), pipeline_mode=once),
                  pl.BlockSpec((1, heads, nkb, V_ROWS, tk), lambda bi, h, i: (bi, h, 0, 0, 0),
                               pipeline_mode=once)],
        out_specs=pl.BlockSpec((1, tq, heads * V_HEAD), lambda bi, h, i: (bi, i, h)),
        out_shape=jax.ShapeDtypeStruct((b, s, D_ATTN), BF16),
        scratch_shapes=[pltpu.VMEM((heads, tk, tq), F32), pltpu.VMEM((heads, tk, tq), F32),
                        pltpu.VMEM((heads, 1, tq), F32), pltpu.VMEM((heads, 1, tq), F32),
                        pltpu.VMEM((heads, V_ROWS, tq), F32), pltpu.VMEM((heads, 1, tq), F32)],
        compiler_params=pltpu.CompilerParams(dimension_semantics=("arbitrary", "arbitrary", "arbitrary"),
                                             vmem_limit_bytes=VMEM_LIMIT_BYTES),
        name="attn_prompt",
    )(qT, qT, k, vT)


def _sattn_kernel(q_ref, latp_ref, krp_ref, latn_ref, krn_ref, w_ukT_ref, w_uv_ref, o_ref, o_scr):
    sd = q_ref.shape[1]
    latp = latp_ref[0].astype(BF16)
    latn = latn_ref[0].astype(BF16)
    krp = krp_ref[0].astype(BF16)
    krn = krn_ref[0].astype(BF16)
    qa, qr = [], []
    for h in range(N_HEADS):
        qh = q_ref[0, :, h * HEAD_SLAB:(h + 1) * HEAD_SLAB]
        qa.append(_dot(qh, w_ukT_ref[h]).astype(BF16))
        qr.append(qh[:, 0:QK_ROPE])
    qa = jnp.concatenate(qa, axis=0)
    qr = jnp.concatenate(qr, axis=0)
    s_p = _dot_nt(qa, latp) + _dot_nt(qr, krp)
    s_n = _dot_nt(qa, latn) + _dot_nt(qr, krn)
    m = jnp.maximum(jnp.max(s_p, axis=-1, keepdims=True), jnp.max(s_n, axis=-1, keepdims=True))
    p_p = jnp.exp(s_p - m)
    p_n = jnp.exp(s_n - m)
    l = jnp.sum(p_p, axis=-1, keepdims=True) + jnp.sum(p_n, axis=-1, keepdims=True)
    o_lat = (_dot(p_p.astype(BF16), latp) + _dot(p_n.astype(BF16), latn)) / l
    o_all = _dot(o_lat.astype(BF16), w_uv_ref[...])
    for h in range(N_HEADS):
        o_scr[:, h * V_HEAD:(h + 1) * V_HEAD] = o_all[h * sd:(h + 1) * sd, h * V_HEAD:(h + 1) * V_HEAD]
    o_ref[0] = o_scr[...].astype(BF16)


def _sattn_call(q, lat_past, kr_past, lat_new, kr_new, w_ukT, w_uv):
    b, sd, _ = q.shape
    p = lat_past.shape[1]
    assert (p + sd - 1) // CHUNK <= p // CHUNK
    return pl.pallas_call(
        _sattn_kernel,
        grid=(b,),
        in_specs=[pl.BlockSpec((1, sd, D_SLAB), lambda i: (i, 0, 0)),
                  pl.BlockSpec((1, p, KV_LORA), lambda i: (i, 0, 0)),
                  pl.BlockSpec((1, p, QK_ROPE), lambda i: (i, 0, 0)),
                  pl.BlockSpec((1, sd, KV_LORA), lambda i: (i, 0, 0)),
                  pl.BlockSpec((1, sd, QK_ROPE), lambda i: (i, 0, 0)),
                  _const_spec((N_HEADS, HEAD_SLAB, KV_LORA)), _const_spec((KV_LORA, D_ATTN))],
        out_specs=pl.BlockSpec((1, sd, D_ATTN), lambda i: (i, 0, 0)),
        out_shape=jax.ShapeDtypeStruct((b, sd, D_ATTN), BF16),
        scratch_shapes=[pltpu.VMEM((sd, D_ATTN), F32)],
        compiler_params=pltpu.CompilerParams(dimension_semantics=("arbitrary",),
                                             vmem_limit_bytes=VMEM_LIMIT_BYTES),
        name="attn_sample",
    )(q, lat_past, kr_past, lat_new, kr_new, w_ukT, w_uv)


def _post_kernel(x_ref, attn_ref, yconv_ref, mod_ref, w_oa_ref, w_oc_ref, gffn_ref,
                 w_gate_ref, w_up_ref, w_down_ref, gfin_ref, y_ref, *, n_chunks):
    nb, ts, d = x_ref.shape
    nr = ts // n_chunks
    rows = nb * nr
    mod = mod_ref[...]
    spans = [slice(c * nr, (c + 1) * nr) for c in range(n_chunks)]

    mixes = []
    for sp in spans:
        a = attn_ref[:, sp, :].reshape(rows, D_ATTN)
        yc = yconv_ref[:, sp, :].reshape(rows, D_CONV)
        mixes.append(_dot(a, w_oa_ref[...]) + _dot(yc, w_oc_ref[...]))
    x1s, gus = [], []
    for sp, mix in zip(spans, mixes):
        x1 = x_ref[:, sp, :] + mod[:, 2:3, :] * mix.reshape(nb, nr, d)
        h = _rms(x1, gffn_ref[...]) * (1.0 + mod[:, 4:5, :]) + mod[:, 3:4, :]
        hb = h.reshape(rows, d).astype(BF16)
        x1s.append(x1)
        gus.append((_dot(hb, w_gate_ref[...]), _dot(hb, w_up_ref[...])))
    ffs = []
    for gate, up in gus:
        act = ((gate * jax.nn.sigmoid(gate)) * up).astype(BF16)
        ffs.append(_dot(act, w_down_ref[...]))
    for sp, x1, ff in zip(spans, x1s, ffs):
        x2 = x1 + mod[:, 5:6, :] * ff.reshape(nb, nr, d)
        y_ref[:, sp, :] = _rms(x2, gfin_ref[...])


def _post_call(x, attn, yconv, mod, w, *, nb, ts, n_chunks=1):
    b, s, d = x.shape
    dff = w["w_gate"].shape[1]
    assert ts % n_chunks == 0 and (ts // n_chunks) % 16 == 0
    tok = lambda width: pl.BlockSpec((nb, ts, width), lambda i, j: (i, j, 0))
    return pl.pallas_call(
        functools.partial(_post_kernel, n_chunks=n_chunks),
        grid=(b // nb, s // ts),
        in_specs=[tok(d), tok(D_ATTN), tok(D_CONV),
                  pl.BlockSpec((nb, 6, d), lambda i, j: (i, 0, 0)),
                  _const_spec((D_ATTN, d)), _const_spec((D_CONV, d)), _const_spec((1, d)),
                  _const_spec((d, dff)), _const_spec((d, dff)), _const_spec((dff, d)),
                  _const_spec((1, d))],
        out_specs=tok(d),
        out_shape=jax.ShapeDtypeStruct((b, s, d), F32),
        compiler_params=pltpu.CompilerParams(dimension_semantics=("arbitrary", "arbitrary"),
                                             vmem_limit_bytes=VMEM_LIMIT_BYTES),
        name="post",
    )(x, attn, yconv, mod, w["w_oa"], w["w_oc"], w["g_ffn"], w["w_gate"], w["w_up"], w["w_down"], w["g_fin"])


def _prep_weights(w_in, w_uq, w_ukv, w_out, norm_mix_g, q_norm_g, kv_norm_g, conv_w, conv_b,
                  norm_ffn_g, w_gate, w_up, w_down, final_norm_g):
    d = w_in.shape[0]
    i1 = Q_LORA
    i2 = i1 + KV_LORA
    i3 = i2 + QK_ROPE
    w_in = w_in.astype(BF16)
    zpad = jnp.zeros((d, HEAD_SLAB - QK_ROPE), BF16)
    k_r = w_in[:, i2:i3]
    k_r_sw = jnp.concatenate([k_r[:, HALF:], k_r[:, :HALF]], axis=1)
    w_in_p = jnp.concatenate([w_in[:, :i2], k_r, zpad, k_r_sw, zpad, w_in[:, i3:]], axis=1)
    assert w_in_p.shape[1] == D_PROJ

    wqT = w_uq.astype(BF16).T.reshape(N_HEADS, QK_NOPE + QK_ROPE, Q_LORA)
    zq = jnp.zeros((N_HEADS, HEAD_SLAB - QK_NOPE - QK_ROPE, Q_LORA), BF16)
    w_uqT = jnp.concatenate([wqT[:, QK_NOPE:], wqT[:, :QK_NOPE], zq], axis=1).reshape(D_SLAB, Q_LORA)
    zq2 = jnp.zeros((N_HEADS, HEAD_SLAB - QK_ROPE, Q_LORA), BF16)
    w_uqsT = jnp.concatenate([wqT[:, QK_NOPE + HALF:], wqT[:, QK_NOPE:QK_NOPE + HALF], zq2],
                             axis=1).reshape(D_SLAB, Q_LORA)
    wkvT = w_ukv.astype(BF16).T.reshape(N_HEADS, QK_NOPE + V_HEAD, KV_LORA)
    zk1 = jnp.zeros((N_HEADS, QK_ROPE, KV_LORA), BF16)
    zk2 = jnp.zeros((N_HEADS, HEAD_SLAB - QK_ROPE - QK_NOPE, KV_LORA), BF16)
    w_ukT = jnp.concatenate([zk1, wkvT[:, :QK_NOPE], zk2], axis=1)
    w_uvT = wkvT[:, QK_NOPE:].reshape(D_ATTN, KV_LORA)

    return {
        "g_mix": norm_mix_g.reshape(1, -1), "w_in": w_in_p,
        "q_g": q_norm_g.reshape(1, -1), "kv_g": kv_norm_g.reshape(1, -1),
        "conv_w": conv_w, "conv_b": conv_b.reshape(1, -1),
        "w_uq": w_uqT.T, "w_uqs": w_uqsT.T, "w_uqT": w_uqT,
        "w_uk": w_ukT.reshape(D_SLAB, KV_LORA).T, "w_ukT": w_ukT,
        "w_uv": w_uvT.T, "w_uvT": w_uvT,
        "w_oa": w_out[:D_ATTN].astype(BF16), "w_oc": w_out[D_ATTN:].astype(BF16),
        "g_ffn": norm_ffn_g.reshape(1, -1),
        "w_gate": w_gate.astype(BF16), "w_up": w_up.astype(BF16), "w_down": w_down.astype(BF16),
        "g_fin": final_norm_g.reshape(1, -1),
    }


def _rope_tables(pos):
    n = pos.shape[0]
    inv_freq = ROPE_THETA ** (-jnp.arange(HALF, dtype=F32) / HALF)
    ang_t = inv_freq[:, None] * pos.astype(F32)[None, :]
    cos_t = jnp.cos(ang_t)
    sin_t = jnp.sin(ang_t)
    z96 = jnp.zeros((HEAD_SLAB - QK_ROPE, n), F32)
    cos128 = jnp.concatenate([cos_t, cos_t, z96], axis=0).T
    sin128 = jnp.concatenate([-sin_t, sin_t, z96], axis=0).T
    return {"cos128": cos128, "sin128": sin128, "cosT": cos_t, "sinT": sin_t}


def _full_width_tables(tabs):
    n = tabs["cos128"].shape[0]
    lane = jnp.arange(HEAD_SLAB)[None, :]
    cosf = jnp.where(lane < QK_ROPE, tabs["cos128"], jnp.ones((n, HEAD_SLAB), F32))
    return {**tabs, "cosf": jnp.tile(cosf, (1, N_HEADS)), "sinf": jnp.tile(tabs["sin128"], (1, N_HEADS))}


def kernel(x_prompt, x_sample, c_prompt, c_sample, cache_kv_latent, cache_k_rope, state_conv,
           w_ada, b_ada, norm_mix_g, w_in, q_norm_g, w_uq, kv_norm_g, w_ukv,
           conv_w, conv_b, w_out, norm_ffn_g, w_gate, w_up, w_down, final_norm_g):
    depth = w_in.shape[0]
    assert depth == 1
    bp, s, d = x_prompt.shape
    bs, sd, _ = x_sample.shape
    p = cache_kv_latent.shape[2]
    l = 0
    w = _prep_weights(w_in[l], w_uq[l], w_ukv[l], w_out[l], norm_mix_g[l], q_norm_g[l], kv_norm_g[l],
                      conv_w[l], conv_b[l], norm_ffn_g[l], w_gate[l], w_up[l], w_down[l], final_norm_g)
    tabs_p = _rope_tables(jnp.arange(s, dtype=jnp.int32))
    tabs_s = _full_width_tables(_rope_tables(p + jnp.arange(sd, dtype=jnp.int32)))

    mod = _mod_call(jnp.concatenate([c_prompt, c_sample], axis=0), w_ada[l], b_ada[l])
    mod = mod.reshape(bp + bs, 6, d)
    mod_p, mod_s = mod[:bp], mod[bp:]

    tq, tk = 512, 256
    conv_zero = jnp.zeros((bp, CONV_W - 1, D_CONV), x_prompt.dtype)
    yconv_p, lat_p, kr_p, cv_p, qT, k_all, vT = _pre_call(
        x_prompt, mod_p, conv_zero, w, tabs_p, nb=1, ts=1024, tk=tk, prompt=True, n_chunks=4)
    attn_p = _attn_call(qT, k_all, vT, tq=tq, tk=tk)
    y_prompt = _post_call(x_prompt, attn_p, yconv_p, mod_p, w, nb=1, ts=1024, n_chunks=4)

    nb_s = 8
    yconv_s, lat_s, kr_s, cv_s, q_s = _pre_call(
        x_sample, mod_s, state_conv[l], w, tabs_s, nb=nb_s, ts=sd, tk=tk, prompt=False)
    attn_s = _sattn_call(q_s, cache_kv_latent[l], cache_k_rope[l], lat_s, kr_s, w["w_ukT"], w["w_uv"])
    y_sample = _post_call(x_sample, attn_s, yconv_s, mod_s, w, nb=nb_s, ts=sd)

    return (y_prompt, y_sample, lat_p[None], kr_p[None], cv_p[None], lat_s[None], kr_s[None], cv_s[None])
```

```python
import functools

import jax
import jax.numpy as jnp
from jax import lax
from jax.experimental import pallas as pl
from jax.experimental.pallas import tpu as pltpu

CHUNK = 64
N_HEADS = 8
QK_NOPE = 64
QK_ROPE = 32
V_HEAD = 64
KV_LORA = 256
Q_LORA = 768
D_CONV = 512
CONV_W = 3
ROPE_THETA = 10000.0
EPS = 1e-6
SM_SCALE = (QK_NOPE + QK_ROPE) ** -0.5
LOG2E = 1.4426950408889634

HALF = QK_ROPE // 2
HEAD_SLAB = 128
D_SLAB = N_HEADS * HEAD_SLAB
D_ATTN = N_HEADS * V_HEAD
V_ROWS = V_HEAD + 16

C_Q = 0
C_KV = C_Q + Q_LORA
C_KR = C_KV + KV_LORA
C_KRS = C_KR + HEAD_SLAB
C_U = C_KRS + HEAD_SLAB
C_GB = C_U + D_CONV
C_GC = C_GB + D_CONV
D_PROJ = C_GC + D_CONV

VMEM_LIMIT_BYTES = 56 * 1024 * 1024

BF16 = jnp.bfloat16
F32 = jnp.float32
NEG = float(jnp.finfo(jnp.float32).min)


def _rms(x, g):
    ms = jnp.mean(x * x, axis=-1, keepdims=True)
    return x * lax.rsqrt(ms + EPS) * g


def _dot(a, b):
    return jnp.dot(a, b, preferred_element_type=F32)


def _dot_nt(a, b):
    return lax.dot_general(a, b, (((1,), (1,)), ((), ())), preferred_element_type=F32)


def _const_spec(shape):
    nd = len(shape)
    return pl.BlockSpec(shape, lambda *_: (0,) * nd, pipeline_mode=pl.Buffered(1))


def _mod_kernel(c_ref, w_ref, b_ref, o_ref):
    c = c_ref[...]
    a = (c * jax.nn.sigmoid(c)).astype(BF16)
    o_ref[...] = _dot(a, w_ref[...].astype(BF16)) + b_ref[...]


def _mod_call(c_all, w_ada, b_ada):
    n, d = c_all.shape
    dn = w_ada.shape[1]
    tn = dn // 4
    return pl.pallas_call(
        _mod_kernel,
        grid=(dn // tn,),
        in_specs=[pl.BlockSpec((n, d), lambda j: (0, 0)),
                  pl.BlockSpec((d, tn), lambda j: (0, j)),
                  pl.BlockSpec((1, tn), lambda j: (0, j))],
        out_specs=pl.BlockSpec((n, tn), lambda j: (0, j)),
        out_shape=jax.ShapeDtypeStruct((n, dn), F32),
        compiler_params=pltpu.CompilerParams(dimension_semantics=("arbitrary",),
                                             vmem_limit_bytes=VMEM_LIMIT_BYTES),
        name="mod",
    )(c_all, w_ada, b_ada.reshape(1, dn))


def _mod_vectors(mod_ref, row0, nb):
    d = mod_ref.shape[1] // 6
    start = row0 + pl.program_id(0) * nb
    if nb % 8 == 0 and row0 % 8 == 0:
        start = pl.multiple_of(start, 8)
    m = mod_ref[pl.ds(start, nb), :]
    return [m[:, k * d:(k + 1) * d][:, None, :] for k in range(6)]


def _pre_norm(x_ref, mod, gmix_ref, r0, nr):
    nb, _, d = x_ref.shape
    h = _rms(x_ref[:, r0:r0 + nr, :], gmix_ref[...]) * (1.0 + mod[1]) + mod[0]
    return h.reshape(nb * nr, d).astype(BF16)


def _pre_mid(proj, r0, nr, kvg_ref, qg_ref, convw_ref, convb_ref, cos128_ref, sin128_ref,
             yconv_ref, lat_ref, krope_ref, carry_ref):
    nb = lat_ref.shape[0]
    latent = _rms(proj[:, C_KV:C_KV + KV_LORA], kvg_ref[...])
    lat_ref[:, r0:r0 + nr, :] = latent.reshape(nb, nr, KV_LORA)

    g1 = proj[:, C_KR:C_KR + HEAD_SLAB].reshape(nb, nr, HEAD_SLAB)
    g2 = proj[:, C_KRS:C_KRS + HEAD_SLAB].reshape(nb, nr, HEAD_SLAB)
    kr128 = g1 * cos128_ref[r0:r0 + nr, :][None] + g2 * sin128_ref[r0:r0 + nr, :][None]
    krope_ref[:, r0:r0 + nr, :] = kr128[:, :, :QK_ROPE]

    u = proj[:, C_U:C_U + D_CONV]
    g_b = proj[:, C_GB:C_GB + D_CONV].reshape(nb, nr, D_CONV)
    g_c = proj[:, C_GC:C_GC + D_CONV]
    gated = (g_c * u).reshape(nb, nr, D_CONV)
    carry_ref[:, 8 + r0:8 + r0 + nr, :] = gated
    prev2 = carry_ref[:, 6 + r0:6 + r0 + nr, :]
    prev1 = carry_ref[:, 7 + r0:7 + r0 + nr, :]
    cw = convw_ref[...]
    conv = cw[0:1][None] * prev2 + cw[1:2][None] * prev1 + cw[2:3][None] * gated + convb_ref[...][None]
    yconv_ref[:, r0:r0 + nr, :] = (g_b * conv).astype(BF16)

    cqn = _rms(proj[:, C_Q:C_Q + Q_LORA], qg_ref[...]).astype(BF16)
    return latent.astype(BF16), kr128, cqn, gated[:, nr - 2:nr, :]


def _pre_prompt_kernel(x_ref, mod_ref, cprev_ref, gmix_ref, w_in_ref, kvg_ref, convw_ref, convb_ref,
                       cos128_ref, sin128_ref, qg_ref, w_uqT_ref, w_uk_ref, w_uvT_ref, cosT_ref, sinT_ref,
                       yconv_ref, lat_ref, krope_ref, cstate_ref, qT_ref, k_ref, vT_ref,
                       carry_ref, *, tk, n_chunks, row0):
    _, ts, _ = x_ref.shape
    nr = ts // n_chunks
    qk = QK_NOPE + QK_ROPE

    @pl.when(pl.program_id(1) == 0)
    def _():
        carry_ref[:, 6:8, :] = cprev_ref[...]

    mod = _mod_vectors(mod_ref, row0, 1)
    hs = [_pre_norm(x_ref, mod, gmix_ref, c * nr, nr) for c in range(n_chunks)]
    projs = [_dot(h, w_in_ref[...]) for h in hs]

    ups = []
    for c in range(n_chunks):
        lat_bf, kr128, cqn, state = _pre_mid(
            projs[c], c * nr, nr, kvg_ref, qg_ref, convw_ref, convb_ref, cos128_ref, sin128_ref,
            yconv_ref, lat_ref, krope_ref, carry_ref)
        qT = _dot_nt(w_uqT_ref[...], cqn) * (SM_SCALE * LOG2E)
        k_nope = _dot(lat_bf, w_uk_ref[...])
        vT = _dot_nt(w_uvT_ref[...], lat_bf)
        ups.append((qT, k_nope, vT, kr128.reshape(nr, HEAD_SLAB)))
    cstate_ref[...] = state
    carry_ref[:, 6:8, :] = state

    step = min(nr, tk)
    pad_rows = jnp.zeros((HEAD_SLAB - qk, nr), BF16)
    for c, (qT, k_nope, vT, kr128) in enumerate(ups):
        r0 = c * nr
        cs = cosT_ref[:, r0:r0 + nr]
        sn = sinT_ref[:, r0:r0 + nr]
        for h in range(N_HEADS):
            b0 = h * HEAD_SLAB
            x1 = qT[h * qk:h * qk + HALF]
            x2 = qT[h * qk + HALF:h * qk + QK_ROPE]
            qT_ref[0, b0:b0 + HALF, r0:r0 + nr] = (x1 * cs - x2 * sn).astype(BF16)
            qT_ref[0, b0 + HALF:b0 + QK_ROPE, r0:r0 + nr] = (x1 * sn + x2 * cs).astype(BF16)
            qT_ref[0, b0 + QK_ROPE:b0 + qk, r0:r0 + nr] = qT[h * qk + QK_ROPE:(h + 1) * qk].astype(BF16)
            qT_ref[0, b0 + qk:b0 + HEAD_SLAB, r0:r0 + nr] = pad_rows
            k_ref[0, r0:r0 + nr, b0:b0 + HEAD_SLAB] = (k_nope[:, b0:b0 + HEAD_SLAB] + kr128).astype(BF16)
            for off in range(0, nr, step):
                cb, lo = divmod(r0 + off, tk)
                vT_ref[0, h, cb, :, lo:lo + step] = vT[h * V_HEAD:(h + 1) * V_HEAD, off:off + step].astype(BF16)


def _pre_sample_kernel(x_ref, mod_ref, cprev_ref, gmix_ref, w_in_ref, kvg_ref, convw_ref, convb_ref,
                       cos128_ref, sin128_ref, qg_ref, w_uq_ref, w_uqs_ref, cosf_ref, sinf_ref,
                       yconv_ref, lat_ref, krope_ref, cstate_ref, q_ref, carry_ref, *, row0):
    nb, ts, _ = x_ref.shape
    carry_ref[:, 6:8, :] = cprev_ref[...]
    mod = _mod_vectors(mod_ref, row0, nb)
    proj = _dot(_pre_norm(x_ref, mod, gmix_ref, 0, ts), w_in_ref[...])
    _, _, cqn, state = _pre_mid(
        proj, 0, ts, kvg_ref, qg_ref, convw_ref, convb_ref, cos128_ref, sin128_ref,
        yconv_ref, lat_ref, krope_ref, carry_ref)
    cstate_ref[...] = state
    q = (_dot(cqn, w_uq_ref[...]) * SM_SCALE).reshape(nb, ts, D_SLAB)
    qs = (_dot(cqn, w_uqs_ref[...]) * SM_SCALE).reshape(nb, ts, D_SLAB)
    q_ref[...] = (q * cosf_ref[...][None] + qs * sinf_ref[...][None]).astype(BF16)


def _pre_call(x, mod, row0, cprev, w, tabs, *, nb, ts, tk, prompt, n_chunks=1):
    b, s, d = x.shape
    grid = (b // nb, s // ts)
    tok = lambda width: pl.BlockSpec((nb, ts, width), lambda i, j: (i, j, 0))
    per_b = lambda rows, width: pl.BlockSpec((nb, rows, width), lambda i, j: (i, 0, 0))
    in_specs = [tok(d), _const_spec(mod.shape), per_b(CONV_W - 1, D_CONV),
                _const_spec((1, d)), _const_spec((d, D_PROJ)), _const_spec((1, KV_LORA)),
                _const_spec((CONV_W, D_CONV)), _const_spec((1, D_CONV)),
                pl.BlockSpec((ts, HEAD_SLAB), lambda i, j: (j, 0)),
                pl.BlockSpec((ts, HEAD_SLAB), lambda i, j: (j, 0)),
                _const_spec((1, Q_LORA))]
    args = [x, mod, cprev, w["g_mix"], w["w_in"], w["kv_g"], w["conv_w"], w["conv_b"],
            tabs["cos128"], tabs["sin128"], w["q_g"]]
    out_specs = [tok(D_CONV), tok(KV_LORA), tok(QK_ROPE), per_b(CONV_W - 1, D_CONV)]
    out_shape = [jax.ShapeDtypeStruct((b, s, D_CONV), BF16),
                 jax.ShapeDtypeStruct((b, s, KV_LORA), F32),
                 jax.ShapeDtypeStruct((b, s, QK_ROPE), F32),
                 jax.ShapeDtypeStruct((b, CONV_W - 1, D_CONV), F32)]
    if prompt:
        nr = ts // n_chunks
        assert nb == 1 and ts % tk == 0 and ts % n_chunks == 0 and nr % HEAD_SLAB == 0
        assert tk % nr == 0 or nr % tk == 0
        kernel = functools.partial(_pre_prompt_kernel, tk=tk, n_chunks=n_chunks, row0=row0)
        in_specs += [_const_spec(w["w_uqT"].shape), _const_spec((KV_LORA, D_SLAB)),
                     _const_spec((D_ATTN, KV_LORA)),
                     pl.BlockSpec((HALF, ts), lambda i, j: (0, j)),
                     pl.BlockSpec((HALF, ts), lambda i, j: (0, j))]
        args += [w["w_uqT"], w["w_uk"], w["w_uvT"], tabs["cosT"], tabs["sinT"]]
        out_specs += [pl.BlockSpec((1, D_SLAB, ts), lambda i, j: (i, 0, j)),
                      tok(D_SLAB),
                      pl.BlockSpec((1, N_HEADS, ts // tk, V_HEAD, tk), lambda i, j: (i, 0, j, 0, 0))]
        out_shape += [jax.ShapeDtypeStruct((b, D_SLAB, s), BF16),
                      jax.ShapeDtypeStruct((b, s, D_SLAB), BF16),
                      jax.ShapeDtypeStruct((b, N_HEADS, s // tk, V_HEAD, tk), BF16)]
    else:
        assert ts == s
        kernel = functools.partial(_pre_sample_kernel, row0=row0)
        in_specs += [_const_spec((Q_LORA, D_SLAB)), _const_spec((Q_LORA, D_SLAB)),
                     _const_spec((ts, D_SLAB)), _const_spec((ts, D_SLAB))]
        args += [w["w_uq"], w["w_uqs"], tabs["cosf"], tabs["sinf"]]
        out_specs += [tok(D_SLAB)]
        out_shape += [jax.ShapeDtypeStruct((b, s, D_SLAB), BF16)]
    return pl.pallas_call(
        kernel, grid=grid, in_specs=in_specs, out_specs=out_specs, out_shape=out_shape,
        scratch_shapes=[pltpu.VMEM((nb, ts + 8, D_CONV), F32)],
        compiler_params=pltpu.CompilerParams(dimension_semantics=("arbitrary", "arbitrary"),
                                             vmem_limit_bytes=VMEM_LIMIT_BYTES),
        name="pre_prompt" if prompt else "pre_sample",
    )(*args)


def _attn_kernel(qT_ref, qT_next_ref, k_ref, vT_ref, o_ref, s0_ref, s1_ref, mb0_ref, mb1_ref, acc_ref, m_ref,
                 *, tq, tk, heads):
    i = pl.program_id(2)
    m_ref[...] = jnp.full(m_ref.shape, NEG, F32)
    acc_ref[...] = jnp.zeros(acc_ref.shape, F32)
    even = (s0_ref, mb0_ref)
    odd = (s1_ref, mb1_ref)
    row = lax.broadcasted_iota(jnp.int32, (V_ROWS - V_HEAD, tk), 0)
    ones_rows = jnp.where(row == 0, 1.0, 0.0).astype(BF16)

    def score_head(j, dst, hh, masked, col0=0, q_ref=qT_ref):
        s_ref, mb_ref = dst
        k0 = pl.multiple_of(j * tk, tk)
        kb = k_ref[0, pl.ds(k0, tk), hh * HEAD_SLAB:(hh + 1) * HEAD_SLAB]
        qT = q_ref[0, hh * HEAD_SLAB:(hh + 1) * HEAD_SLAB, col0:tq]
        sT = _dot(kb, qT)
        if masked:
            kpos = k0 + lax.broadcasted_iota(jnp.int32, (tk, 1), 0)
            qpos = i * tq + col0 + lax.broadcasted_iota(jnp.int32, (1, tq - col0), 1)
            sT = jnp.where(kpos <= (qpos | (CHUNK - 1)), sT, NEG)
        s_ref[hh, :, col0:tq] = sT
        mb_ref[hh, :, col0:tq] = jnp.max(sT, axis=0, keepdims=True)

    def fold_head(j, src, hh, col0=0):
        s_ref, mb_ref = src
        m_old = m_ref[hh, :, col0:tq]
        m_new = jnp.maximum(m_old, mb_ref[hh, :, col0:tq])
        alpha = jnp.exp2(m_old - m_new)
        pT = jnp.exp2(s_ref[hh, :, col0:tq] - m_new).astype(BF16)
        pv = _dot(jnp.concatenate([vT_ref[0, hh, j], ones_rows], axis=0), pT)
        acc_ref[hh, :, col0:tq] = acc_ref[hh, :, col0:tq] * alpha + pv
        m_ref[hh, :, col0:tq] = m_new

    def stage(j, src, dst, masked, col0=0):
        for hh in range(heads):
            score_head(j + 1, dst, hh, masked, col0)
            fold_head(j, src, hh)

    jd = 2 * i

    @pl.when(i == 0)
    def _():
        for hh in range(heads):
            score_head(0, even, hh, True)

    def pair(j):
        stage(j, even, odd, False)
        stage(j + 1, odd, even, False)

    def body(jj, carry):
        pair(4 * jj)
        pair(4 * jj + 2)
        return carry

    n_pairs = jnp.maximum(i - 1, 0)
    lax.fori_loop(0, n_pairs >> 1, body, 0)

    @pl.when((n_pairs & 1) == 1)
    def _():
        pair(2 * n_pairs - 2)

    @pl.when(i > 0)
    def _():
        stage(jd - 2, even, odd, False)
        stage(jd - 1, odd, even, True)

    stage(jd, even, odd, True, tk)
    for hh in range(heads):
        score_head(0, even, hh, False, q_ref=qT_next_ref)
        fold_head(jd + 1, odd, hh, tk)

    outs = []
    for hh in range(heads):
        acc = acc_ref[hh]
        outs.append(acc[0:V_HEAD] / acc[V_HEAD:V_HEAD + 1])
    oT = jnp.concatenate(outs, axis=0)
    o_ref[0] = oT.T.astype(BF16)


def _attn_call(qT, k, vT, *, tq, tk, heads=N_HEADS):
    b, _, s = qT.shape
    nkb = s // tk
    assert tk % CHUNK == 0 and tq == 2 * tk and s % tq == 0 and N_HEADS % heads == 0
    n_tiles = s // tq
    grid = (b, N_HEADS // heads, n_tiles)
    once = pl.Buffered(1)
    return pl.pallas_call(
        functools.partial(_attn_kernel, tq=tq, tk=tk, heads=heads),
        grid=grid,
        in_specs=[pl.BlockSpec((1, heads * HEAD_SLAB, tq), lambda bi, h, i: (bi, h, i)),
                  pl.BlockSpec((1, heads * HEAD_SLAB, tq),
                               lambda bi, h, i: (bi, h, jnp.minimum(i + 1, n_tiles - 1))),
                  pl.BlockSpec((1, s, heads * HEAD_SLAB), lambda bi, h, i: (bi, 0, h), pipeline_mode=once),
                  pl.BlockSpec((1, heads, nkb, V_HEAD, tk), lambda bi, h, i: (bi, h, 0, 0, 0),
                               pipeline_mode=once)],
        out_specs=pl.BlockSpec((1, tq, heads * V_HEAD), lambda bi, h, i: (bi, i, h)),
        out_shape=jax.ShapeDtypeStruct((b, s, D_ATTN), BF16),
        scratch_shapes=[pltpu.VMEM((heads, tk, tq), F32), pltpu.VMEM((heads, tk, tq), F32),
                        pltpu.VMEM((heads, 1, tq), F32), pltpu.VMEM((heads, 1, tq), F32),
                        pltpu.VMEM((heads, V_ROWS, tq), F32), pltpu.VMEM((heads, 1, tq), F32)],
        compiler_params=pltpu.CompilerParams(dimension_semantics=("arbitrary", "arbitrary", "arbitrary"),
                                             vmem_limit_bytes=VMEM_LIMIT_BYTES),
        name="attn_prompt",
    )(qT, qT, k, vT)


def _sattn_kernel(q_ref, latp_ref, krp_ref, latn_ref, krn_ref, w_ukT_ref, w_uv_ref, o_ref, o_scr):
    sd = q_ref.shape[1]
    latp = latp_ref[0].astype(BF16)
    latn = latn_ref[0].astype(BF16)
    krp = krp_ref[0].astype(BF16)
    krn = krn_ref[0].astype(BF16)
    qa, qr = [], []
    for h in range(N_HEADS):
        qh = q_ref[0, :, h * HEAD_SLAB:(h + 1) * HEAD_SLAB]
        qa.append(_dot(qh, w_ukT_ref[h]).astype(BF16))
        qr.append(qh[:, 0:QK_ROPE])
    qa = jnp.concatenate(qa, axis=0)
    qr = jnp.concatenate(qr, axis=0)
    s_p = _dot_nt(qa, latp) + _dot_nt(qr, krp)
    s_n = _dot_nt(qa, latn) + _dot_nt(qr, krn)
    m = jnp.maximum(jnp.max(s_p, axis=-1, keepdims=True), jnp.max(s_n, axis=-1, keepdims=True))
    p_p = jnp.exp(s_p - m)
    p_n = jnp.exp(s_n - m)
    l = jnp.sum(p_p, axis=-1, keepdims=True) + jnp.sum(p_n, axis=-1, keepdims=True)
    o_lat = (_dot(p_p.astype(BF16), latp) + _dot(p_n.astype(BF16), latn)) / l
    o_all = _dot(o_lat.astype(BF16), w_uv_ref[...])
    for h in range(N_HEADS):
        o_scr[:, h * V_HEAD:(h + 1) * V_HEAD] = o_all[h * sd:(h + 1) * sd, h * V_HEAD:(h + 1) * V_HEAD]
    o_ref[0] = o_scr[...].astype(BF16)


def _sattn_call(q, lat_past, kr_past, lat_new, kr_new, w_ukT, w_uv):
    b, sd, _ = q.shape
    p = lat_past.shape[1]
    assert (p + sd - 1) // CHUNK <= p // CHUNK
    return pl.pallas_call(
        _sattn_kernel,
        grid=(b,),
        in_specs=[pl.BlockSpec((1, sd, D_SLAB), lambda i: (i, 0, 0)),
                  pl.BlockSpec((1, p, KV_LORA), lambda i: (i, 0, 0)),
                  pl.BlockSpec((1, p, QK_ROPE), lambda i: (i, 0, 0)),
                  pl.BlockSpec((1, sd, KV_LORA), lambda i: (i, 0, 0)),
                  pl.BlockSpec((1, sd, QK_ROPE), lambda i: (i, 0, 0)),
                  _const_spec((N_HEADS, HEAD_SLAB, KV_LORA)), _const_spec((KV_LORA, D_ATTN))],
        out_specs=pl.BlockSpec((1, sd, D_ATTN), lambda i: (i, 0, 0)),
        out_shape=jax.ShapeDtypeStruct((b, sd, D_ATTN), BF16),
        scratch_shapes=[pltpu.VMEM((sd, D_ATTN), F32)],
        compiler_params=pltpu.CompilerParams(dimension_semantics=("arbitrary",),
                                             vmem_limit_bytes=VMEM_LIMIT_BYTES),
        name="attn_sample",
    )(q, lat_past, kr_past, lat_new, kr_new, w_ukT, w_uv)


def _post_kernel(x_ref, attn_ref, yconv_ref, mod_ref, w_oa_ref, w_oc_ref, gffn_ref,
                 w_gate_ref, w_up_ref, w_down_ref, gfin_ref, y_ref, *, n_chunks, row0):
    nb, ts, d = x_ref.shape
    nr = ts // n_chunks
    rows = nb * nr
    _, _, gate1, shift2, scale2, gate2 = _mod_vectors(mod_ref, row0, nb)
    spans = [slice(c * nr, (c + 1) * nr) for c in range(n_chunks)]

    mixes = []
    for sp in spans:
        a = attn_ref[:, sp, :].reshape(rows, D_ATTN)
        yc = yconv_ref[:, sp, :].reshape(rows, D_CONV)
        mixes.append(_dot(a, w_oa_ref[...]) + _dot(yc, w_oc_ref[...]))
    x1s, gus = [], []
    for sp, mix in zip(spans, mixes):
        x1 = x_ref[:, sp, :] + gate1 * mix.reshape(nb, nr, d)
        h = _rms(x1, gffn_ref[...]) * (1.0 + scale2) + shift2
        hb = h.reshape(rows, d).astype(BF16)
        x1s.append(x1)
        gus.append((_dot(hb, w_gate_ref[...]), _dot(hb, w_up_ref[...])))
    ffs = []
    for gate, up in gus:
        act = ((gate * jax.nn.sigmoid(gate)) * up).astype(BF16)
        ffs.append(_dot(act, w_down_ref[...]))
    for sp, x1, ff in zip(spans, x1s, ffs):
        x2 = x1 + gate2 * ff.reshape(nb, nr, d)
        y_ref[:, sp, :] = _rms(x2, gfin_ref[...])


def _post_call(x, attn, yconv, mod, row0, w, *, nb, ts, n_chunks=1):
    b, s, d = x.shape
    dff = w["w_gate"].shape[1]
    assert ts % n_chunks == 0 and (ts // n_chunks) % 16 == 0
    tok = lambda width: pl.BlockSpec((nb, ts, width), lambda i, j: (i, j, 0))
    return pl.pallas_call(
        functools.partial(_post_kernel, n_chunks=n_chunks, row0=row0),
        grid=(b // nb, s // ts),
        in_specs=[tok(d), tok(D_ATTN), tok(D_CONV),
                  _const_spec(mod.shape),
                  _const_spec((D_ATTN, d)), _const_spec((D_CONV, d)), _const_spec((1, d)),
                  _const_spec((d, dff)), _const_spec((d, dff)), _const_spec((dff, d)),
                  _const_spec((1, d))],
        out_specs=tok(d),
        out_shape=jax.ShapeDtypeStruct((b, s, d), F32),
        compiler_params=pltpu.CompilerParams(dimension_semantics=("arbitrary", "arbitrary"),
                                             vmem_limit_bytes=VMEM_LIMIT_BYTES),
        name="post",
    )(x, attn, yconv, mod, w["w_oa"], w["w_oc"], w["g_ffn"], w["w_gate"], w["w_up"], w["w_down"], w["g_fin"])


def _prep_weights(w_in, w_uq, w_ukv, w_out, norm_mix_g, q_norm_g, kv_norm_g, conv_w, conv_b,
                  norm_ffn_g, w_gate, w_up, w_down, final_norm_g):
    d = w_in.shape[0]
    i1 = Q_LORA
    i2 = i1 + KV_LORA
    i3 = i2 + QK_ROPE
    w_in = w_in.astype(BF16)
    zpad = jnp.zeros((d, HEAD_SLAB - QK_ROPE), BF16)
    k_r = w_in[:, i2:i3]
    k_r_sw = jnp.concatenate([k_r[:, HALF:], k_r[:, :HALF]], axis=1)
    w_in_p = jnp.concatenate([w_in[:, :i2], k_r, zpad, k_r_sw, zpad, w_in[:, i3:]], axis=1)
    assert w_in_p.shape[1] == D_PROJ

    wqT = w_uq.astype(BF16).T.reshape(N_HEADS, QK_NOPE + QK_ROPE, Q_LORA)
    zq = jnp.zeros((N_HEADS, HEAD_SLAB - QK_NOPE - QK_ROPE, Q_LORA), BF16)
    w_uqT = jnp.concatenate([wqT[:, QK_NOPE:], wqT[:, :QK_NOPE], zq], axis=1).reshape(D_SLAB, Q_LORA)
    zq2 = jnp.zeros((N_HEADS, HEAD_SLAB - QK_ROPE, Q_LORA), BF16)
    w_uqsT = jnp.concatenate([wqT[:, QK_NOPE + HALF:], wqT[:, QK_NOPE:QK_NOPE + HALF], zq2],
                             axis=1).reshape(D_SLAB, Q_LORA)
    wkvT = w_ukv.astype(BF16).T.reshape(N_HEADS, QK_NOPE + V_HEAD, KV_LORA)
    zk1 = jnp.zeros((N_HEADS, QK_ROPE, KV_LORA), BF16)
    zk2 = jnp.zeros((N_HEADS, HEAD_SLAB - QK_ROPE - QK_NOPE, KV_LORA), BF16)
    w_ukT = jnp.concatenate([zk1, wkvT[:, :QK_NOPE], zk2], axis=1)
    w_uvT = wkvT[:, QK_NOPE:].reshape(D_ATTN, KV_LORA)

    return {
        "g_mix": norm_mix_g.reshape(1, -1), "w_in": w_in_p,
        "q_g": q_norm_g.reshape(1, -1), "kv_g": kv_norm_g.reshape(1, -1),
        "conv_w": conv_w, "conv_b": conv_b.reshape(1, -1),
        "w_uq": w_uqT.T, "w_uqs": w_uqsT.T,
        "w_uqT": jnp.concatenate([wqT[:, QK_NOPE:], wqT[:, :QK_NOPE]], axis=1).reshape(-1, Q_LORA),
        "w_uk": w_ukT.reshape(D_SLAB, KV_LORA).T, "w_ukT": w_ukT,
        "w_uv": w_uvT.T, "w_uvT": w_uvT,
        "w_oa": w_out[:D_ATTN].astype(BF16), "w_oc": w_out[D_ATTN:].astype(BF16),
        "g_ffn": norm_ffn_g.reshape(1, -1),
        "w_gate": w_gate.astype(BF16), "w_up": w_up.astype(BF16), "w_down": w_down.astype(BF16),
        "g_fin": final_norm_g.reshape(1, -1),
    }


def _rope_tables(pos):
    n = pos.shape[0]
    inv_freq = ROPE_THETA ** (-jnp.arange(HALF, dtype=F32) / HALF)
    ang_t = inv_freq[:, None] * pos.astype(F32)[None, :]
    cos_t = jnp.cos(ang_t)
    sin_t = jnp.sin(ang_t)
    z96 = jnp.zeros((HEAD_SLAB - QK_ROPE, n), F32)
    cos128 = jnp.concatenate([cos_t, cos_t, z96], axis=0).T
    sin128 = jnp.concatenate([-sin_t, sin_t, z96], axis=0).T
    return {"cos128": cos128, "sin128": sin128, "cosT": cos_t, "sinT": sin_t}


def _full_width_tables(tabs):
    n = tabs["cos128"].shape[0]
    lane = jnp.arange(HEAD_SLAB)[None, :]
    cosf = jnp.where(lane < QK_ROPE, tabs["cos128"], jnp.ones((n, HEAD_SLAB), F32))
    return {**tabs, "cosf": jnp.tile(cosf, (1, N_HEADS)), "sinf": jnp.tile(tabs["sin128"], (1, N_HEADS))}


def kernel(x_prompt, x_sample, c_prompt, c_sample, cache_kv_latent, cache_k_rope, state_conv,
           w_ada, b_ada, norm_mix_g, w_in, q_norm_g, w_uq, kv_norm_g, w_ukv,
           conv_w, conv_b, w_out, norm_ffn_g, w_gate, w_up, w_down, final_norm_g):
    depth = w_in.shape[0]
    assert depth == 1
    bp, s, d = x_prompt.shape
    bs, sd, _ = x_sample.shape
    p = cache_kv_latent.shape[2]
    l = 0
    w = _prep_weights(w_in[l], w_uq[l], w_ukv[l], w_out[l], norm_mix_g[l], q_norm_g[l], kv_norm_g[l],
                      conv_w[l], conv_b[l], norm_ffn_g[l], w_gate[l], w_up[l], w_down[l], final_norm_g)
    tabs_p = _rope_tables(jnp.arange(s, dtype=jnp.int32))
    tabs_s = _full_width_tables(_rope_tables(p + jnp.arange(sd, dtype=jnp.int32)))

    mod = _mod_call(jnp.concatenate([c_prompt, c_sample], axis=0), w_ada[l], b_ada[l])

    tq, tk = 512, 256
    conv_zero = jnp.zeros((bp, CONV_W - 1, D_CONV), x_prompt.dtype)
    yconv_p, lat_p, kr_p, cv_p, qT, k_all, vT = _pre_call(
        x_prompt, mod, 0, conv_zero, w, tabs_p, nb=1, ts=1024, tk=tk, prompt=True, n_chunks=4)
    attn_p = _attn_call(qT, k_all, vT, tq=tq, tk=tk)
    y_prompt = _post_call(x_prompt, attn_p, yconv_p, mod, 0, w, nb=1, ts=1024, n_chunks=4)

    nb_s = 8
    yconv_s, lat_s, kr_s, cv_s, q_s = _pre_call(
        x_sample, mod, bp, state_conv[l], w, tabs_s, nb=nb_s, ts=sd, tk=tk, prompt=False)
    attn_s = _sattn_call(q_s, cache_kv_latent[l], cache_k_rope[l], lat_s, kr_s, w["w_ukT"], w["w_uv"])
    y_sample = _post_call(x_sample, attn_s, yconv_s, mod, bp, w, nb=nb_s, ts=sd)

    return (y_prompt, y_sample, lat_p[None], kr_p[None], cv_p[None], lat_s[None], kr_s[None], cv_s[None])
```

```python
import functools

import jax
import jax.numpy as jnp
from jax import lax
from jax.experimental import pallas as pl
from jax.experimental.pallas import tpu as pltpu

CHUNK = 64
N_HEADS = 8
QK_NOPE = 64
QK_ROPE = 32
V_HEAD = 64
KV_LORA = 256
Q_LORA = 768
D_CONV = 512
CONV_W = 3
ROPE_THETA = 10000.0
EPS = 1e-6
SM_SCALE = (QK_NOPE + QK_ROPE) ** -0.5
LOG2E = 1.4426950408889634

HALF = QK_ROPE // 2
HEAD_SLAB = 128
D_SLAB = N_HEADS * HEAD_SLAB
D_ATTN = N_HEADS * V_HEAD
V_ROWS = V_HEAD + 16

C_Q = 0
C_KV = C_Q + Q_LORA
C_KR = C_KV + KV_LORA
C_KRS = C_KR + HEAD_SLAB
C_U = C_KRS + HEAD_SLAB
C_GB = C_U + D_CONV
C_GC = C_GB + D_CONV
D_PROJ = C_GC + D_CONV

VMEM_LIMIT_BYTES = 56 * 1024 * 1024

BF16 = jnp.bfloat16
F32 = jnp.float32
NEG = float(jnp.finfo(jnp.float32).min)


def _rms(x, g):
    ms = jnp.mean(x * x, axis=-1, keepdims=True)
    return x * lax.rsqrt(ms + EPS) * g


def _dot(a, b):
    return jnp.dot(a, b, preferred_element_type=F32)


def _dot_nt(a, b):
    return lax.dot_general(a, b, (((1,), (1,)), ((), ())), preferred_element_type=F32)


def _const_spec(shape):
    nd = len(shape)
    return pl.BlockSpec(shape, lambda *_: (0,) * nd, pipeline_mode=pl.Buffered(1))


def _mod_kernel(c_ref, w_ref, b_ref, o_ref):
    c = c_ref[...]
    a = (c * jax.nn.sigmoid(c)).astype(BF16)
    o_ref[...] = _dot(a, w_ref[...].astype(BF16)) + b_ref[...]


def _mod_call(c_all, w_ada, b_ada):
    n, d = c_all.shape
    dn = w_ada.shape[1]
    tn = dn // 4
    return pl.pallas_call(
        _mod_kernel,
        grid=(dn // tn,),
        in_specs=[pl.BlockSpec((n, d), lambda j: (0, 0)),
                  pl.BlockSpec((d, tn), lambda j: (0, j)),
                  pl.BlockSpec((1, tn), lambda j: (0, j))],
        out_specs=pl.BlockSpec((n, tn), lambda j: (0, j)),
        out_shape=jax.ShapeDtypeStruct((n, dn), F32),
        compiler_params=pltpu.CompilerParams(dimension_semantics=("arbitrary",),
                                             vmem_limit_bytes=VMEM_LIMIT_BYTES),
        name="mod",
    )(c_all, w_ada, b_ada.reshape(1, dn))


def _mod_vectors(mod_ref, row0, nb):
    d = mod_ref.shape[1] // 6
    start = row0 + pl.program_id(0) * nb
    if nb % 8 == 0 and row0 % 8 == 0:
        start = pl.multiple_of(start, 8)
    m = mod_ref[pl.ds(start, nb), :]
    return [m[:, k * d:(k + 1) * d][:, None, :] for k in range(6)]


def _pre_norm(x_ref, mod, gmix_ref, r0, nr):
    nb, _, d = x_ref.shape
    h = _rms(x_ref[:, r0:r0 + nr, :], gmix_ref[...]) * (1.0 + mod[1]) + mod[0]
    return h.reshape(nb * nr, d).astype(BF16)


def _pre_mid(proj, r0, nr, kvg_ref, qg_ref, convw_ref, convb_ref, cos128, sin128,
             yconv_ref, lat_ref, carry_ref):
    nb = lat_ref.shape[0]
    latent = _rms(proj[:, C_KV:C_KV + KV_LORA], kvg_ref[...])
    lat_ref[:, r0:r0 + nr, :] = latent.reshape(nb, nr, KV_LORA)

    g1 = proj[:, C_KR:C_KR + HEAD_SLAB].reshape(nb, nr, HEAD_SLAB)
    g2 = proj[:, C_KRS:C_KRS + HEAD_SLAB].reshape(nb, nr, HEAD_SLAB)
    kr128 = g1 * cos128[None] + g2 * sin128[None]

    u = proj[:, C_U:C_U + D_CONV]
    g_b = proj[:, C_GB:C_GB + D_CONV].reshape(nb, nr, D_CONV)
    g_c = proj[:, C_GC:C_GC + D_CONV]
    gated = (g_c * u).reshape(nb, nr, D_CONV)
    carry_ref[:, 8 + r0:8 + r0 + nr, :] = gated
    prev2 = carry_ref[:, 6 + r0:6 + r0 + nr, :]
    prev1 = carry_ref[:, 7 + r0:7 + r0 + nr, :]
    cw = convw_ref[...]
    conv = cw[0:1][None] * prev2 + cw[1:2][None] * prev1 + cw[2:3][None] * gated + convb_ref[...][None]
    yconv_ref[:, r0:r0 + nr, :] = (g_b * conv).astype(BF16)

    cqn = _rms(proj[:, C_Q:C_Q + Q_LORA], qg_ref[...]).astype(BF16)
    return latent.astype(BF16), kr128, cqn, gated[:, nr - 2:nr, :]


def _pre_prompt_kernel(x_ref, mod_ref, cprev_ref, gmix_ref, w_in_ref, kvg_ref, convw_ref, convb_ref,
                       qg_ref, w_uqT_ref, w_uk_ref, w_uvT_ref, cosT_ref, sinT_ref,
                       yconv_ref, lat_ref, kropeT_ref, cstate_ref, qT_ref, k_ref, vT_ref,
                       carry_ref, *, tk, n_chunks, row0):
    _, ts, _ = x_ref.shape
    nr = ts // n_chunks
    qk = QK_NOPE + QK_ROPE

    @pl.when(pl.program_id(1) == 0)
    def _():
        carry_ref[:, 6:8, :] = cprev_ref[...]

    mod = _mod_vectors(mod_ref, row0, 1)
    hs = [_pre_norm(x_ref, mod, gmix_ref, c * nr, nr) for c in range(n_chunks)]
    projs = [_dot(h, w_in_ref[...]) for h in hs]

    ups = []
    zrows = jnp.zeros((HEAD_SLAB - QK_ROPE, nr), F32)
    for c in range(n_chunks):
        r0 = c * nr
        cs = cosT_ref[:, r0:r0 + nr]
        sn = sinT_ref[:, r0:r0 + nr]
        cos128 = jnp.concatenate([cs, cs, zrows], axis=0).T
        sin128 = jnp.concatenate([-sn, sn, zrows], axis=0).T
        lat_bf, kr128, cqn, state = _pre_mid(
            projs[c], r0, nr, kvg_ref, qg_ref, convw_ref, convb_ref, cos128, sin128,
            yconv_ref, lat_ref, carry_ref)
        kr128 = kr128.reshape(nr, HEAD_SLAB)
        kropeT_ref[0, :, r0:r0 + nr] = kr128.T[0:QK_ROPE]
        qT = _dot_nt(w_uqT_ref[...], cqn) * (SM_SCALE * LOG2E)
        k_nope = _dot(lat_bf, w_uk_ref[...])
        vT = _dot_nt(w_uvT_ref[...], lat_bf)
        ups.append((qT, k_nope, vT, kr128))
    cstate_ref[...] = state
    carry_ref[:, 6:8, :] = state

    step = min(nr, tk)
    pad_rows = jnp.zeros((HEAD_SLAB - qk, nr), BF16)
    for c, (qT, k_nope, vT, kr128) in enumerate(ups):
        r0 = c * nr
        cs = cosT_ref[:, r0:r0 + nr]
        sn = sinT_ref[:, r0:r0 + nr]
        for h in range(N_HEADS):
            b0 = h * HEAD_SLAB
            x1 = qT[h * qk:h * qk + HALF]
            x2 = qT[h * qk + HALF:h * qk + QK_ROPE]
            qT_ref[0, b0:b0 + HALF, r0:r0 + nr] = (x1 * cs - x2 * sn).astype(BF16)
            qT_ref[0, b0 + HALF:b0 + QK_ROPE, r0:r0 + nr] = (x1 * sn + x2 * cs).astype(BF16)
            qT_ref[0, b0 + QK_ROPE:b0 + qk, r0:r0 + nr] = qT[h * qk + QK_ROPE:(h + 1) * qk].astype(BF16)
            qT_ref[0, b0 + qk:b0 + HEAD_SLAB, r0:r0 + nr] = pad_rows
            k_ref[0, r0:r0 + nr, b0:b0 + HEAD_SLAB] = (k_nope[:, b0:b0 + HEAD_SLAB] + kr128).astype(BF16)
            for off in range(0, nr, step):
                cb, lo = divmod(r0 + off, tk)
                vT_ref[0, h, cb, :, lo:lo + step] = vT[h * V_HEAD:(h + 1) * V_HEAD, off:off + step].astype(BF16)


def _pre_sample_kernel(x_ref, mod_ref, cprev_ref, gmix_ref, w_in_ref, kvg_ref, convw_ref, convb_ref,
                       cos128_ref, sin128_ref, qg_ref, w_uq_ref, w_uqs_ref, cosf_ref, sinf_ref,
                       yconv_ref, lat_ref, krope_ref, cstate_ref, q_ref, carry_ref, *, row0):
    nb, ts, _ = x_ref.shape
    carry_ref[:, 6:8, :] = cprev_ref[...]
    mod = _mod_vectors(mod_ref, row0, nb)
    proj = _dot(_pre_norm(x_ref, mod, gmix_ref, 0, ts), w_in_ref[...])
    _, kr128, cqn, state = _pre_mid(
        proj, 0, ts, kvg_ref, qg_ref, convw_ref, convb_ref, cos128_ref[...], sin128_ref[...],
        yconv_ref, lat_ref, carry_ref)
    krope_ref[...] = kr128[:, :, :QK_ROPE]
    cstate_ref[...] = state
    q = (_dot(cqn, w_uq_ref[...]) * SM_SCALE).reshape(nb, ts, D_SLAB)
    qs = (_dot(cqn, w_uqs_ref[...]) * SM_SCALE).reshape(nb, ts, D_SLAB)
    q_ref[...] = (q * cosf_ref[...][None] + qs * sinf_ref[...][None]).astype(BF16)


def _pre_call(x, mod, row0, cprev, w, tabs, *, nb, ts, tk, prompt, n_chunks=1):
    b, s, d = x.shape
    grid = (b // nb, s // ts)
    tok = lambda width: pl.BlockSpec((nb, ts, width), lambda i, j: (i, j, 0))
    per_b = lambda rows, width: pl.BlockSpec((nb, rows, width), lambda i, j: (i, 0, 0))
    in_specs = [tok(d), _const_spec(mod.shape), per_b(CONV_W - 1, D_CONV),
                _const_spec((1, d)), _const_spec((d, D_PROJ)), _const_spec((1, KV_LORA)),
                _const_spec((CONV_W, D_CONV)), _const_spec((1, D_CONV))]
    args = [x, mod, cprev, w["g_mix"], w["w_in"], w["kv_g"], w["conv_w"], w["conv_b"]]
    krope_spec = pl.BlockSpec((1, QK_ROPE, ts), lambda i, j: (i, 0, j)) if prompt else tok(QK_ROPE)
    krope_shape = (b, QK_ROPE, s) if prompt else (b, s, QK_ROPE)
    out_specs = [tok(D_CONV), tok(KV_LORA), krope_spec, per_b(CONV_W - 1, D_CONV)]
    out_shape = [jax.ShapeDtypeStruct((b, s, D_CONV), BF16),
                 jax.ShapeDtypeStruct((b, s, KV_LORA), F32),
                 jax.ShapeDtypeStruct(krope_shape, F32),
                 jax.ShapeDtypeStruct((b, CONV_W - 1, D_CONV), F32)]
    if prompt:
        nr = ts // n_chunks
        assert nb == 1 and ts % tk == 0 and ts % n_chunks == 0 and nr % HEAD_SLAB == 0
        assert tk % nr == 0 or nr % tk == 0
        kernel = functools.partial(_pre_prompt_kernel, tk=tk, n_chunks=n_chunks, row0=row0)
        in_specs += [_const_spec((1, Q_LORA)), _const_spec(w["w_uqT"].shape), _const_spec((KV_LORA, D_SLAB)),
                     _const_spec((D_ATTN, KV_LORA)),
                     pl.BlockSpec((HALF, ts), lambda i, j: (0, j)),
                     pl.BlockSpec((HALF, ts), lambda i, j: (0, j))]
        args += [w["q_g"], w["w_uqT"], w["w_uk"], w["w_uvT"], tabs["cosT"], tabs["sinT"]]
        out_specs += [pl.BlockSpec((1, D_SLAB, ts), lambda i, j: (i, 0, j)),
                      tok(D_SLAB),
                      pl.BlockSpec((1, N_HEADS, ts // tk, V_HEAD, tk), lambda i, j: (i, 0, j, 0, 0))]
        out_shape += [jax.ShapeDtypeStruct((b, D_SLAB, s), BF16),
                      jax.ShapeDtypeStruct((b, s, D_SLAB), BF16),
                      jax.ShapeDtypeStruct((b, N_HEADS, s // tk, V_HEAD, tk), BF16)]
    else:
        assert ts == s
        kernel = functools.partial(_pre_sample_kernel, row0=row0)
        in_specs += [_const_spec((ts, HEAD_SLAB)), _const_spec((ts, HEAD_SLAB)), _const_spec((1, Q_LORA)),
                     _const_spec((Q_LORA, D_SLAB)), _const_spec((Q_LORA, D_SLAB)),
                     _const_spec((ts, D_SLAB)), _const_spec((ts, D_SLAB))]
        args += [tabs["cos128"], tabs["sin128"], w["q_g"], w["w_uq"], w["w_uqs"], tabs["cosf"], tabs["sinf"]]
        out_specs += [tok(D_SLAB)]
        out_shape += [jax.ShapeDtypeStruct((b, s, D_SLAB), BF16)]
    return pl.pallas_call(
        kernel, grid=grid, in_specs=in_specs, out_specs=out_specs, out_shape=out_shape,
        scratch_shapes=[pltpu.VMEM((nb, ts + 8, D_CONV), F32)],
        compiler_params=pltpu.CompilerParams(dimension_semantics=("arbitrary", "arbitrary"),
                                             vmem_limit_bytes=VMEM_LIMIT_BYTES),
        name="pre_prompt" if prompt else "pre_sample",
    )(*args)


def _attn_kernel(qT_ref, qT_next_ref, k_ref, vT_ref, o_ref, s0_ref, s1_ref, mb0_ref, mb1_ref, acc_ref, m_ref,
                 *, tq, tk, heads):
    i = pl.program_id(2)
    m_ref[...] = jnp.full(m_ref.shape, NEG, F32)
    acc_ref[...] = jnp.zeros(acc_ref.shape, F32)
    even = (s0_ref, mb0_ref)
    odd = (s1_ref, mb1_ref)
    row = lax.broadcasted_iota(jnp.int32, (V_ROWS - V_HEAD, tk), 0)
    ones_rows = jnp.where(row == 0, 1.0, 0.0).astype(BF16)

    def score_head(j, dst, hh, masked, col0=0, q_ref=qT_ref):
        s_ref, mb_ref = dst
        k0 = pl.multiple_of(j * tk, tk)
        kb = k_ref[0, pl.ds(k0, tk), hh * HEAD_SLAB:(hh + 1) * HEAD_SLAB]
        qT = q_ref[0, hh * HEAD_SLAB:(hh + 1) * HEAD_SLAB, col0:tq]
        sT = _dot(kb, qT)
        if masked:
            kpos = k0 + lax.broadcasted_iota(jnp.int32, (tk, 1), 0)
            qpos = i * tq + col0 + lax.broadcasted_iota(jnp.int32, (1, tq - col0), 1)
            sT = jnp.where(kpos <= (qpos | (CHUNK - 1)), sT, NEG)
        s_ref[hh, :, col0:tq] = sT
        mb_ref[hh, :, col0:tq] = jnp.max(sT, axis=0, keepdims=True)

    def fold_head(j, src, hh, col0=0):
        s_ref, mb_ref = src
        m_old = m_ref[hh, :, col0:tq]
        m_new = jnp.maximum(m_old, mb_ref[hh, :, col0:tq])
        alpha = jnp.exp2(m_old - m_new)
        pT = jnp.exp2(s_ref[hh, :, col0:tq] - m_new).astype(BF16)
        pv = _dot(jnp.concatenate([vT_ref[0, hh, j], ones_rows], axis=0), pT)
        acc_ref[hh, :, col0:tq] = acc_ref[hh, :, col0:tq] * alpha + pv
        m_ref[hh, :, col0:tq] = m_new

    def stage(j, src, dst, masked, col0=0):
        for hh in range(heads):
            score_head(j + 1, dst, hh, masked, col0)
            fold_head(j, src, hh)

    jd = 2 * i

    @pl.when(i == 0)
    def _():
        for hh in range(heads):
            score_head(0, even, hh, True)

    def pair(j):
        stage(j, even, odd, False)
        stage(j + 1, odd, even, False)

    def body(jj, carry):
        pair(4 * jj)
        pair(4 * jj + 2)
        return carry

    n_pairs = jnp.maximum(i - 1, 0)
    lax.fori_loop(0, n_pairs >> 1, body, 0)

    @pl.when((n_pairs & 1) == 1)
    def _():
        pair(2 * n_pairs - 2)

    @pl.when(i > 0)
    def _():
        stage(jd - 2, even, odd, False)
        stage(jd - 1, odd, even, True)

    stage(jd, even, odd, True, tk)
    for hh in range(heads):
        score_head(0, even, hh, False, q_ref=qT_next_ref)
        fold_head(jd + 1, odd, hh, tk)

    outs = []
    for hh in range(heads):
        acc = acc_ref[hh]
        outs.append(acc[0:V_HEAD] / acc[V_HEAD:V_HEAD + 1])
    oT = jnp.concatenate(outs, axis=0)
    o_ref[0] = oT.T.astype(BF16)


def _attn_call(qT, k, vT, *, tq, tk, heads=N_HEADS):
    b, _, s = qT.shape
    nkb = s // tk
    assert tk % CHUNK == 0 and tq == 2 * tk and s % tq == 0 and N_HEADS % heads == 0
    n_tiles = s // tq
    grid = (b, N_HEADS // heads, n_tiles)
    once = pl.Buffered(1)
    return pl.pallas_call(
        functools.partial(_attn_kernel, tq=tq, tk=tk, heads=heads),
        grid=grid,
        in_specs=[pl.BlockSpec((1, heads * HEAD_SLAB, tq), lambda bi, h, i: (bi, h, i)),
                  pl.BlockSpec((1, heads * HEAD_SLAB, tq),
                               lambda bi, h, i: (bi, h, jnp.minimum(i + 1, n_tiles - 1))),
                  pl.BlockSpec((1, s, heads * HEAD_SLAB), lambda bi, h, i: (bi, 0, h), pipeline_mode=once),
                  pl.BlockSpec((1, heads, nkb, V_HEAD, tk), lambda bi, h, i: (bi, h, 0, 0, 0),
                               pipeline_mode=once)],
        out_specs=pl.BlockSpec((1, tq, heads * V_HEAD), lambda bi, h, i: (bi, i, h)),
        out_shape=jax.ShapeDtypeStruct((b, s, D_ATTN), BF16),
        scratch_shapes=[pltpu.VMEM((heads, tk, tq), F32), pltpu.VMEM((heads, tk, tq), F32),
                        pltpu.VMEM((heads, 1, tq), F32), pltpu.VMEM((heads, 1, tq), F32),
                        pltpu.VMEM((heads, V_ROWS, tq), F32), pltpu.VMEM((heads, 1, tq), F32)],
        compiler_params=pltpu.CompilerParams(dimension_semantics=("arbitrary", "arbitrary", "arbitrary"),
                                             vmem_limit_bytes=VMEM_LIMIT_BYTES),
        name="attn_prompt",
    )(qT, qT, k, vT)


def _sattn_kernel(q_ref, latp_ref, krp_ref, latn_ref, krn_ref, w_ukT_ref, w_uv_ref, o_ref, o_scr):
    sd = q_ref.shape[1]
    latp = latp_ref[0].astype(BF16)
    latn = latn_ref[0].astype(BF16)
    krp_t = krp_ref[0].astype(BF16)
    krn = krn_ref[0].astype(BF16)
    qa, qr = [], []
    for h in range(N_HEADS):
        qh = q_ref[0, :, h * HEAD_SLAB:(h + 1) * HEAD_SLAB]
        qa.append(_dot(qh, w_ukT_ref[h]).astype(BF16))
        qr.append(qh[:, 0:QK_ROPE])
    qa = jnp.concatenate(qa, axis=0)
    qr = jnp.concatenate(qr, axis=0)
    s_p = _dot_nt(qa, latp) + _dot(qr, krp_t)
    s_n = _dot_nt(qa, latn) + _dot_nt(qr, krn)
    m = jnp.maximum(jnp.max(s_p, axis=-1, keepdims=True), jnp.max(s_n, axis=-1, keepdims=True))
    p_p = jnp.exp(s_p - m)
    p_n = jnp.exp(s_n - m)
    l = jnp.sum(p_p, axis=-1, keepdims=True) + jnp.sum(p_n, axis=-1, keepdims=True)
    o_lat = (_dot(p_p.astype(BF16), latp) + _dot(p_n.astype(BF16), latn)) / l
    o_all = _dot(o_lat.astype(BF16), w_uv_ref[...])
    for h in range(N_HEADS):
        o_scr[:, h * V_HEAD:(h + 1) * V_HEAD] = o_all[h * sd:(h + 1) * sd, h * V_HEAD:(h + 1) * V_HEAD]
    o_ref[0] = o_scr[...].astype(BF16)


def _sattn_call(q, lat_past, kr_past_t, lat_new, kr_new, w_ukT, w_uv):
    b, sd, _ = q.shape
    p = lat_past.shape[1]
    assert (p + sd - 1) // CHUNK <= p // CHUNK
    return pl.pallas_call(
        _sattn_kernel,
        grid=(b,),
        in_specs=[pl.BlockSpec((1, sd, D_SLAB), lambda i: (i, 0, 0)),
                  pl.BlockSpec((1, p, KV_LORA), lambda i: (i, 0, 0)),
                  pl.BlockSpec((1, QK_ROPE, p), lambda i: (i, 0, 0)),
                  pl.BlockSpec((1, sd, KV_LORA), lambda i: (i, 0, 0)),
                  pl.BlockSpec((1, sd, QK_ROPE), lambda i: (i, 0, 0)),
                  _const_spec((N_HEADS, HEAD_SLAB, KV_LORA)), _const_spec((KV_LORA, D_ATTN))],
        out_specs=pl.BlockSpec((1, sd, D_ATTN), lambda i: (i, 0, 0)),
        out_shape=jax.ShapeDtypeStruct((b, sd, D_ATTN), BF16),
        scratch_shapes=[pltpu.VMEM((sd, D_ATTN), F32)],
        compiler_params=pltpu.CompilerParams(dimension_semantics=("arbitrary",),
                                             vmem_limit_bytes=VMEM_LIMIT_BYTES),
        name="attn_sample",
    )(q, lat_past, kr_past_t, lat_new, kr_new, w_ukT, w_uv)


def _post_kernel(x_ref, attn_ref, yconv_ref, mod_ref, w_oa_ref, w_oc_ref, gffn_ref,
                 w_gate_ref, w_up_ref, w_down_ref, gfin_ref, y_ref, *, n_chunks, row0):
    nb, ts, d = x_ref.shape
    nr = ts // n_chunks
    rows = nb * nr
    _, _, gate1, shift2, scale2, gate2 = _mod_vectors(mod_ref, row0, nb)
    spans = [slice(c * nr, (c + 1) * nr) for c in range(n_chunks)]

    mixes = []
    for sp in spans:
        a = attn_ref[:, sp, :].reshape(rows, D_ATTN)
        yc = yconv_ref[:, sp, :].reshape(rows, D_CONV)
        mixes.append(_dot(a, w_oa_ref[...]) + _dot(yc, w_oc_ref[...]))
    x1s, gus = [], []
    for sp, mix in zip(spans, mixes):
        x1 = x_ref[:, sp, :] + gate1 * mix.reshape(nb, nr, d)
        h = _rms(x1, gffn_ref[...]) * (1.0 + scale2) + shift2
        hb = h.reshape(rows, d).astype(BF16)
        x1s.append(x1)
        gus.append((_dot(hb, w_gate_ref[...]), _dot(hb, w_up_ref[...])))
    ffs = []
    for gate, up in gus:
        act = ((gate * jax.nn.sigmoid(gate)) * up).astype(BF16)
        ffs.append(_dot(act, w_down_ref[...]))
    for sp, x1, ff in zip(spans, x1s, ffs):
        x2 = x1 + gate2 * ff.reshape(nb, nr, d)
        y_ref[:, sp, :] = _rms(x2, gfin_ref[...])


def _post_call(x, attn, yconv, mod, row0, w, *, nb, ts, n_chunks=1):
    b, s, d = x.shape
    dff = w["w_gate"].shape[1]
    assert ts % n_chunks == 0 and (ts // n_chunks) % 16 == 0
    tok = lambda width: pl.BlockSpec((nb, ts, width), lambda i, j: (i, j, 0))
    return pl.pallas_call(
        functools.partial(_post_kernel, n_chunks=n_chunks, row0=row0),
        grid=(b // nb, s // ts),
        in_specs=[tok(d), tok(D_ATTN), tok(D_CONV),
                  _const_spec(mod.shape),
                  _const_spec((D_ATTN, d)), _const_spec((D_CONV, d)), _const_spec((1, d)),
                  _const_spec((d, dff)), _const_spec((d, dff)), _const_spec((dff, d)),
                  _const_spec((1, d))],
        out_specs=tok(d),
        out_shape=jax.ShapeDtypeStruct((b, s, d), F32),
        compiler_params=pltpu.CompilerParams(dimension_semantics=("arbitrary", "arbitrary"),
                                             vmem_limit_bytes=VMEM_LIMIT_BYTES),
        name="post",
    )(x, attn, yconv, mod, w["w_oa"], w["w_oc"], w["g_ffn"], w["w_gate"], w["w_up"], w["w_down"], w["g_fin"])


def _prep_weights(w_in, w_uq, w_ukv, w_out, norm_mix_g, q_norm_g, kv_norm_g, conv_w, conv_b,
                  norm_ffn_g, w_gate, w_up, w_down, final_norm_g):
    d = w_in.shape[0]
    i1 = Q_LORA
    i2 = i1 + KV_LORA
    i3 = i2 + QK_ROPE
    w_in_t = w_in.astype(BF16).T
    zpad = jnp.zeros((HEAD_SLAB - QK_ROPE, d), BF16)
    k_r = w_in_t[i2:i3]
    w_in_p = jnp.concatenate([w_in_t[:i2], k_r, zpad, k_r[HALF:], k_r[:HALF], zpad, w_in_t[i3:]], axis=0).T
    assert w_in_p.shape[1] == D_PROJ

    wqT = w_uq.astype(BF16).T.reshape(N_HEADS, QK_NOPE + QK_ROPE, Q_LORA)
    zq = jnp.zeros((N_HEADS, HEAD_SLAB - QK_NOPE - QK_ROPE, Q_LORA), BF16)
    w_uqT = jnp.concatenate([wqT[:, QK_NOPE:], wqT[:, :QK_NOPE], zq], axis=1).reshape(D_SLAB, Q_LORA)
    zq2 = jnp.zeros((N_HEADS, HEAD_SLAB - QK_ROPE, Q_LORA), BF16)
    w_uqsT = jnp.concatenate([wqT[:, QK_NOPE + HALF:], wqT[:, QK_NOPE:QK_NOPE + HALF], zq2],
                             axis=1).reshape(D_SLAB, Q_LORA)
    wkvT = w_ukv.astype(BF16).T.reshape(N_HEADS, QK_NOPE + V_HEAD, KV_LORA)
    zk1 = jnp.zeros((N_HEADS, QK_ROPE, KV_LORA), BF16)
    zk2 = jnp.zeros((N_HEADS, HEAD_SLAB - QK_ROPE - QK_NOPE, KV_LORA), BF16)
    w_ukT = jnp.concatenate([zk1, wkvT[:, :QK_NOPE], zk2], axis=1)
    w_uvT = wkvT[:, QK_NOPE:].reshape(D_ATTN, KV_LORA)

    return {
        "g_mix": norm_mix_g.reshape(1, -1), "w_in": w_in_p,
        "q_g": q_norm_g.reshape(1, -1), "kv_g": kv_norm_g.reshape(1, -1),
        "conv_w": conv_w, "conv_b": conv_b.reshape(1, -1),
        "w_uq": w_uqT.T, "w_uqs": w_uqsT.T,
        "w_uqT": jnp.concatenate([wqT[:, QK_NOPE:], wqT[:, :QK_NOPE]], axis=1).reshape(-1, Q_LORA),
        "w_uk": w_ukT.reshape(D_SLAB, KV_LORA).T, "w_ukT": w_ukT,
        "w_uv": w_uvT.T, "w_uvT": w_uvT,
        "w_oa": w_out[:D_ATTN].astype(BF16), "w_oc": w_out[D_ATTN:].astype(BF16),
        "g_ffn": norm_ffn_g.reshape(1, -1),
        "w_gate": w_gate.astype(BF16), "w_up": w_up.astype(BF16), "w_down": w_down.astype(BF16),
        "g_fin": final_norm_g.reshape(1, -1),
    }


def _rope_tables(pos):
    n = pos.shape[0]
    inv_freq = ROPE_THETA ** (-jnp.arange(HALF, dtype=F32) / HALF)
    ang_t = inv_freq[:, None] * pos.astype(F32)[None, :]
    cos_t = jnp.cos(ang_t)
    sin_t = jnp.sin(ang_t)
    z96 = jnp.zeros((HEAD_SLAB - QK_ROPE, n), F32)
    cos128 = jnp.concatenate([cos_t, cos_t, z96], axis=0).T
    sin128 = jnp.concatenate([-sin_t, sin_t, z96], axis=0).T
    return {"cos128": cos128, "sin128": sin128, "cosT": cos_t, "sinT": sin_t}


def _full_width_tables(tabs):
    n = tabs["cos128"].shape[0]
    lane = jnp.arange(HEAD_SLAB)[None, :]
    cosf = jnp.where(lane < QK_ROPE, tabs["cos128"], jnp.ones((n, HEAD_SLAB), F32))
    return {**tabs, "cosf": jnp.tile(cosf, (1, N_HEADS)), "sinf": jnp.tile(tabs["sin128"], (1, N_HEADS))}


def kernel(x_prompt, x_sample, c_prompt, c_sample, cache_kv_latent, cache_k_rope, state_conv,
           w_ada, b_ada, norm_mix_g, w_in, q_norm_g, w_uq, kv_norm_g, w_ukv,
           conv_w, conv_b, w_out, norm_ffn_g, w_gate, w_up, w_down, final_norm_g):
    depth = w_in.shape[0]
    assert depth == 1
    bp, s, d = x_prompt.shape
    bs, sd, _ = x_sample.shape
    p = cache_kv_latent.shape[2]
    l = 0
    w = _prep_weights(w_in[l], w_uq[l], w_ukv[l], w_out[l], norm_mix_g[l], q_norm_g[l], kv_norm_g[l],
                      conv_w[l], conv_b[l], norm_ffn_g[l], w_gate[l], w_up[l], w_down[l], final_norm_g)
    tabs_p = _rope_tables(jnp.arange(s, dtype=jnp.int32))
    tabs_s = _full_width_tables(_rope_tables(p + jnp.arange(sd, dtype=jnp.int32)))

    mod = _mod_call(jnp.concatenate([c_prompt, c_sample], axis=0), w_ada[l], b_ada[l])

    tq, tk = 512, 256
    conv_zero = jnp.zeros((bp, CONV_W - 1, D_CONV), x_prompt.dtype)
    yconv_p, lat_p, kr_p_t, cv_p, qT, k_all, vT = _pre_call(
        x_prompt, mod, 0, conv_zero, w, tabs_p, nb=1, ts=1024, tk=tk, prompt=True, n_chunks=4)
    attn_p = _attn_call(qT, k_all, vT, tq=tq, tk=tk)
    y_prompt = _post_call(x_prompt, attn_p, yconv_p, mod, 0, w, nb=1, ts=1024, n_chunks=4)

    nb_s = 8
    yconv_s, lat_s, kr_s, cv_s, q_s = _pre_call(
        x_sample, mod, bp, state_conv[l], w, tabs_s, nb=nb_s, ts=sd, tk=tk, prompt=False)
    attn_s = _sattn_call(q_s, cache_kv_latent[l], jnp.swapaxes(cache_k_rope[l], 1, 2), lat_s, kr_s,
                         w["w_ukT"], w["w_uv"])
    y_sample = _post_call(x_sample, attn_s, yconv_s, mod, bp, w, nb=nb_s, ts=sd)

    kr_p = jnp.swapaxes(kr_p_t, 1, 2)
    return (y_prompt, y_sample, lat_p[None], kr_p[None], cv_p[None], lat_s[None], kr_s[None], cv_s[None])
```

```python
import functools

import jax
import jax.numpy as jnp
from jax import lax
from jax.experimental import pallas as pl
from jax.experimental.pallas import tpu as pltpu

CHUNK = 64
N_HEADS = 8
QK_NOPE = 64
QK_ROPE = 32
V_HEAD = 64
KV_LORA = 256
Q_LORA = 768
D_CONV = 512
CONV_W = 3
ROPE_THETA = 10000.0
EPS = 1e-6
SM_SCALE = (QK_NOPE + QK_ROPE) ** -0.5
LOG2E = 1.4426950408889634

HALF = QK_ROPE // 2
HEAD_SLAB = 128
D_SLAB = N_HEADS * HEAD_SLAB
D_ATTN = N_HEADS * V_HEAD
V_ROWS = V_HEAD + 16

C_Q = 0
C_KV = C_Q + Q_LORA
C_KR = C_KV + KV_LORA
C_KRS = C_KR + HEAD_SLAB
C_U = C_KRS + HEAD_SLAB
C_GB = C_U + D_CONV
C_GC = C_GB + D_CONV
D_PROJ = C_GC + D_CONV

VMEM_LIMIT_BYTES = 56 * 1024 * 1024
ATTN_VMEM_LIMIT_BYTES = 58 * 1024 * 1024

BF16 = jnp.bfloat16
F32 = jnp.float32
NEG = float(jnp.finfo(jnp.float32).min)


def _rms(x, g):
    ms = jnp.mean(x * x, axis=-1, keepdims=True)
    return x * lax.rsqrt(ms + EPS) * g


def _dot(a, b):
    return jnp.dot(a, b, preferred_element_type=F32)


def _dot_nt(a, b):
    return lax.dot_general(a, b, (((1,), (1,)), ((), ())), preferred_element_type=F32)


def _const_spec(shape):
    nd = len(shape)
    return pl.BlockSpec(shape, lambda *_: (0,) * nd, pipeline_mode=pl.Buffered(1))


def _mod_kernel(c_ref, w_ref, b_ref, o_ref):
    c = c_ref[...]
    a = (c * jax.nn.sigmoid(c)).astype(BF16)
    o_ref[...] = _dot(a, w_ref[...].astype(BF16)) + b_ref[...]


def _mod_call(c_all, w_ada, b_ada):
    n, d = c_all.shape
    dn = w_ada.shape[1]
    tn = dn // 4
    return pl.pallas_call(
        _mod_kernel,
        grid=(dn // tn,),
        in_specs=[pl.BlockSpec((n, d), lambda j: (0, 0)),
                  pl.BlockSpec((d, tn), lambda j: (0, j)),
                  pl.BlockSpec((1, tn), lambda j: (0, j))],
        out_specs=pl.BlockSpec((n, tn), lambda j: (0, j)),
        out_shape=jax.ShapeDtypeStruct((n, dn), F32),
        compiler_params=pltpu.CompilerParams(dimension_semantics=("arbitrary",),
                                             vmem_limit_bytes=VMEM_LIMIT_BYTES),
        name="mod",
    )(c_all, w_ada, b_ada.reshape(1, dn))


def _mod_vectors(mod_ref, row0, nb):
    d = mod_ref.shape[1] // 6
    start = row0 + pl.program_id(0) * nb
    if nb % 8 == 0 and row0 % 8 == 0:
        start = pl.multiple_of(start, 8)
    m = mod_ref[pl.ds(start, nb), :]
    return [m[:, k * d:(k + 1) * d][:, None, :] for k in range(6)]


def _pre_norm(x_ref, mod, gmix_ref, r0, nr):
    nb, _, d = x_ref.shape
    h = _rms(x_ref[:, r0:r0 + nr, :], gmix_ref[...]) * (1.0 + mod[1]) + mod[0]
    return h.reshape(nb * nr, d).astype(BF16)


def _pre_mid(proj, r0, nr, kvg_ref, qg_ref, convw_ref, convb_ref, cos128, sin128,
             yconv_ref, lat_ref, carry_ref):
    nb = lat_ref.shape[0]
    latent = _rms(proj[:, C_KV:C_KV + KV_LORA], kvg_ref[...])
    lat_ref[:, r0:r0 + nr, :] = latent.reshape(nb, nr, KV_LORA)

    g1 = proj[:, C_KR:C_KR + HEAD_SLAB].reshape(nb, nr, HEAD_SLAB)
    g2 = proj[:, C_KRS:C_KRS + HEAD_SLAB].reshape(nb, nr, HEAD_SLAB)
    kr128 = g1 * cos128[None] + g2 * sin128[None]

    u = proj[:, C_U:C_U + D_CONV]
    g_b = proj[:, C_GB:C_GB + D_CONV].reshape(nb, nr, D_CONV)
    g_c = proj[:, C_GC:C_GC + D_CONV]
    gated = (g_c * u).reshape(nb, nr, D_CONV)
    carry_ref[:, 8 + r0:8 + r0 + nr, :] = gated
    prev2 = carry_ref[:, 6 + r0:6 + r0 + nr, :]
    prev1 = carry_ref[:, 7 + r0:7 + r0 + nr, :]
    cw = convw_ref[...]
    conv = cw[0:1][None] * prev2 + cw[1:2][None] * prev1 + cw[2:3][None] * gated + convb_ref[...][None]
    yconv_ref[:, r0:r0 + nr, :] = (g_b * conv).astype(BF16)

    cqn = _rms(proj[:, C_Q:C_Q + Q_LORA], qg_ref[...]).astype(BF16)
    return latent.astype(BF16), kr128, cqn, gated[:, nr - 2:nr, :]


def _pre_prompt_kernel(x_ref, mod_ref, cprev_ref, gmix_ref, w_in_ref, kvg_ref, convw_ref, convb_ref,
                       qg_ref, w_uqT_ref, w_uk_ref, w_uvT_ref, cosT_ref, sinT_ref,
                       yconv_ref, lat_ref, kropeT_ref, cstate_ref, qT_ref, k_ref, vT_ref,
                       carry_ref, *, tk, n_chunks, row0):
    _, ts, _ = x_ref.shape
    nr = ts // n_chunks
    qk = QK_NOPE + QK_ROPE

    @pl.when(pl.program_id(1) == 0)
    def _():
        carry_ref[:, 6:8, :] = cprev_ref[...]

    mod = _mod_vectors(mod_ref, row0, 1)
    hs = [_pre_norm(x_ref, mod, gmix_ref, c * nr, nr) for c in range(n_chunks)]
    projs = [_dot(h, w_in_ref[...]) for h in hs]

    ups = []
    zrows = jnp.zeros((HEAD_SLAB - QK_ROPE, nr), F32)
    for c in range(n_chunks):
        r0 = c * nr
        cs = cosT_ref[:, r0:r0 + nr]
        sn = sinT_ref[:, r0:r0 + nr]
        cos128 = jnp.concatenate([cs, cs, zrows], axis=0).T
        sin128 = jnp.concatenate([-sn, sn, zrows], axis=0).T
        lat_bf, kr128, cqn, state = _pre_mid(
            projs[c], r0, nr, kvg_ref, qg_ref, convw_ref, convb_ref, cos128, sin128,
            yconv_ref, lat_ref, carry_ref)
        kr128 = kr128.reshape(nr, HEAD_SLAB)
        kropeT_ref[0, :, r0:r0 + nr] = kr128.T[0:QK_ROPE]
        qT = _dot_nt(w_uqT_ref[...], cqn) * (SM_SCALE * LOG2E)
        k_nope = _dot(lat_bf, w_uk_ref[...])
        vT = _dot_nt(w_uvT_ref[...], lat_bf)
        ups.append((qT, k_nope, vT, kr128))
    cstate_ref[...] = state
    carry_ref[:, 6:8, :] = state

    step = min(nr, tk)
    pad_rows = jnp.zeros((HEAD_SLAB - qk, nr), BF16)
    for c, (qT, k_nope, vT, kr128) in enumerate(ups):
        r0 = c * nr
        cs = cosT_ref[:, r0:r0 + nr]
        sn = sinT_ref[:, r0:r0 + nr]
        for h in range(N_HEADS):
            b0 = h * HEAD_SLAB
            x1 = qT[h * qk:h * qk + HALF]
            x2 = qT[h * qk + HALF:h * qk + QK_ROPE]
            qT_ref[0, b0:b0 + HALF, r0:r0 + nr] = (x1 * cs - x2 * sn).astype(BF16)
            qT_ref[0, b0 + HALF:b0 + QK_ROPE, r0:r0 + nr] = (x1 * sn + x2 * cs).astype(BF16)
            qT_ref[0, b0 + QK_ROPE:b0 + qk, r0:r0 + nr] = qT[h * qk + QK_ROPE:(h + 1) * qk].astype(BF16)
            qT_ref[0, b0 + qk:b0 + HEAD_SLAB, r0:r0 + nr] = pad_rows
            k_ref[0, r0:r0 + nr, b0:b0 + HEAD_SLAB] = (k_nope[:, b0:b0 + HEAD_SLAB] + kr128).astype(BF16)
            for off in range(0, nr, step):
                cb, lo = divmod(r0 + off, tk)
                vT_ref[0, h, cb, :, lo:lo + step] = vT[h * V_HEAD:(h + 1) * V_HEAD, off:off + step].astype(BF16)


def _pre_sample_kernel(x_ref, mod_ref, cprev_ref, gmix_ref, w_in_ref, kvg_ref, convw_ref, convb_ref,
                       cos128_ref, sin128_ref, qg_ref, w_uq_ref, w_uqs_ref, cosf_ref, sinf_ref,
                       yconv_ref, lat_ref, krope_ref, cstate_ref, q_ref, carry_ref, *, row0):
    nb, ts, _ = x_ref.shape
    carry_ref[:, 6:8, :] = cprev_ref[...]
    mod = _mod_vectors(mod_ref, row0, nb)
    proj = _dot(_pre_norm(x_ref, mod, gmix_ref, 0, ts), w_in_ref[...])
    _, kr128, cqn, state = _pre_mid(
        proj, 0, ts, kvg_ref, qg_ref, convw_ref, convb_ref, cos128_ref[...], sin128_ref[...],
        yconv_ref, lat_ref, carry_ref)
    krope_ref[...] = kr128[:, :, :QK_ROPE]
    cstate_ref[...] = state
    q = (_dot(cqn, w_uq_ref[...]) * SM_SCALE).reshape(nb, ts, D_SLAB)
    qs = (_dot(cqn, w_uqs_ref[...]) * SM_SCALE).reshape(nb, ts, D_SLAB)
    q_ref[...] = (q * cosf_ref[...][None] + qs * sinf_ref[...][None]).astype(BF16)


def _pre_call(x, mod, row0, cprev, w, tabs, *, nb, ts, tk, prompt, n_chunks=1):
    b, s, d = x.shape
    grid = (b // nb, s // ts)
    tok = lambda width: pl.BlockSpec((nb, ts, width), lambda i, j: (i, j, 0))
    per_b = lambda rows, width: pl.BlockSpec((nb, rows, width), lambda i, j: (i, 0, 0))
    in_specs = [tok(d), _const_spec(mod.shape), per_b(CONV_W - 1, D_CONV),
                _const_spec((1, d)), _const_spec((d, D_PROJ)), _const_spec((1, KV_LORA)),
                _const_spec((CONV_W, D_CONV)), _const_spec((1, D_CONV))]
    args = [x, mod, cprev, w["g_mix"], w["w_in"], w["kv_g"], w["conv_w"], w["conv_b"]]
    krope_spec = pl.BlockSpec((1, QK_ROPE, ts), lambda i, j: (i, 0, j)) if prompt else tok(QK_ROPE)
    krope_shape = (b, QK_ROPE, s) if prompt else (b, s, QK_ROPE)
    out_specs = [tok(D_CONV), tok(KV_LORA), krope_spec, per_b(CONV_W - 1, D_CONV)]
    out_shape = [jax.ShapeDtypeStruct((b, s, D_CONV), BF16),
                 jax.ShapeDtypeStruct((b, s, KV_LORA), F32),
                 jax.ShapeDtypeStruct(krope_shape, F32),
                 jax.ShapeDtypeStruct((b, CONV_W - 1, D_CONV), F32)]
    if prompt:
        nr = ts // n_chunks
        assert nb == 1 and ts % tk == 0 and ts % n_chunks == 0 and nr % HEAD_SLAB == 0
        assert tk % nr == 0 or nr % tk == 0
        kernel = functools.partial(_pre_prompt_kernel, tk=tk, n_chunks=n_chunks, row0=row0)
        in_specs += [_const_spec((1, Q_LORA)), _const_spec(w["w_uqT"].shape), _const_spec((KV_LORA, D_SLAB)),
                     _const_spec((D_ATTN, KV_LORA)),
                     pl.BlockSpec((HALF, ts), lambda i, j: (0, j)),
                     pl.BlockSpec((HALF, ts), lambda i, j: (0, j))]
        args += [w["q_g"], w["w_uqT"], w["w_uk"], w["w_uvT"], tabs["cosT"], tabs["sinT"]]
        out_specs += [pl.BlockSpec((1, D_SLAB, ts), lambda i, j: (i, 0, j)),
                      tok(D_SLAB),
                      pl.BlockSpec((1, N_HEADS, ts // tk, V_HEAD, tk), lambda i, j: (i, 0, j, 0, 0))]
        out_shape += [jax.ShapeDtypeStruct((b, D_SLAB, s), BF16),
                      jax.ShapeDtypeStruct((b, s, D_SLAB), BF16),
                      jax.ShapeDtypeStruct((b, N_HEADS, s // tk, V_HEAD, tk), BF16)]
    else:
        assert ts == s
        kernel = functools.partial(_pre_sample_kernel, row0=row0)
        in_specs += [_const_spec((ts, HEAD_SLAB)), _const_spec((ts, HEAD_SLAB)), _const_spec((1, Q_LORA)),
                     _const_spec((Q_LORA, D_SLAB)), _const_spec((Q_LORA, D_SLAB)),
                     _const_spec((ts, D_SLAB)), _const_spec((ts, D_SLAB))]
        args += [tabs["cos128"], tabs["sin128"], w["q_g"], w["w_uq"], w["w_uqs"], tabs["cosf"], tabs["sinf"]]
        out_specs += [tok(D_SLAB)]
        out_shape += [jax.ShapeDtypeStruct((b, s, D_SLAB), BF16)]
    return pl.pallas_call(
        kernel, grid=grid, in_specs=in_specs, out_specs=out_specs, out_shape=out_shape,
        scratch_shapes=[pltpu.VMEM((nb, ts + 8, D_CONV), F32)],
        compiler_params=pltpu.CompilerParams(dimension_semantics=("arbitrary", "arbitrary"),
                                             vmem_limit_bytes=VMEM_LIMIT_BYTES),
        name="pre_prompt" if prompt else "pre_sample",
    )(*args)


def _attn_kernel(qT_ref, qT_next_ref, k_ref, vT_ref, o_ref, s0_ref, s1_ref, mb0_ref, mb1_ref, acc_ref, m_ref,
                 *, tq, tk, heads):
    i = pl.program_id(2)
    m_ref[...] = jnp.full(m_ref.shape, NEG, F32)
    acc_ref[...] = jnp.zeros(acc_ref.shape, F32)
    even = (s0_ref, mb0_ref)
    odd = (s1_ref, mb1_ref)
    row = lax.broadcasted_iota(jnp.int32, (V_ROWS - V_HEAD, tk), 0)
    ones_rows = jnp.where(row == 0, 1.0, 0.0).astype(BF16)

    def score_head(j, dst, hh, masked, col0=0, q_ref=qT_ref):
        s_ref, mb_ref = dst
        k0 = pl.multiple_of(j * tk, tk)
        kb = k_ref[0, pl.ds(k0, tk), hh * HEAD_SLAB:(hh + 1) * HEAD_SLAB]
        qT = q_ref[0, hh * HEAD_SLAB:(hh + 1) * HEAD_SLAB, col0:tq]
        sT = _dot(kb, qT)
        if masked:
            kpos = k0 + lax.broadcasted_iota(jnp.int32, (tk, 1), 0)
            qpos = i * tq + col0 + lax.broadcasted_iota(jnp.int32, (1, tq - col0), 1)
            sT = jnp.where(kpos <= (qpos | (CHUNK - 1)), sT, NEG)
        s_ref[hh, :, col0:tq] = sT
        mb_ref[hh, :, col0:tq] = jnp.max(sT, axis=0, keepdims=True)

    def fold_head(j, src, hh, col0=0):
        s_ref, mb_ref = src
        m_old = m_ref[hh, :, col0:tq]
        m_new = jnp.maximum(m_old, mb_ref[hh, :, col0:tq])
        alpha = jnp.exp2(m_old - m_new)
        pT = jnp.exp2(s_ref[hh, :, col0:tq] - m_new).astype(BF16)
        pv = _dot(jnp.concatenate([vT_ref[0, hh, j], ones_rows], axis=0), pT)
        acc_ref[hh, :, col0:tq] = acc_ref[hh, :, col0:tq] * alpha + pv
        m_ref[hh, :, col0:tq] = m_new

    def stage(j, src, dst, masked, col0=0):
        for hh in range(heads):
            score_head(j + 1, dst, hh, masked, col0)
            fold_head(j, src, hh)

    jd = 2 * i

    @pl.when(i == 0)
    def _():
        for hh in range(heads):
            score_head(0, even, hh, True)

    def pair(j):
        stage(j, even, odd, False)
        stage(j + 1, odd, even, False)

    def body(jj, carry):
        pair(4 * jj)
        pair(4 * jj + 2)
        return carry

    n_pairs = jnp.maximum(i - 1, 0)
    lax.fori_loop(0, n_pairs >> 1, body, 0)

    @pl.when((n_pairs & 1) == 1)
    def _():
        pair(2 * n_pairs - 2)

    @pl.when(i > 0)
    def _():
        stage(jd - 2, even, odd, False)
        stage(jd - 1, odd, even, True)

    stage(jd, even, odd, True, tk)
    for hh in range(heads):
        score_head(0, even, hh, False, q_ref=qT_next_ref)
        fold_head(jd + 1, odd, hh, tk)

    outs = []
    for hh in range(heads):
        acc = acc_ref[hh]
        outs.append(acc[0:V_HEAD] / acc[V_HEAD:V_HEAD + 1])
    oT = jnp.concatenate(outs, axis=0)
    o_ref[0] = oT.T.astype(BF16)


def _attn_call(qT, k, vT, *, tq, tk, heads=N_HEADS):
    b, _, s = qT.shape
    nkb = s // tk
    assert tk % CHUNK == 0 and tq == 2 * tk and s % tq == 0 and N_HEADS % heads == 0
    n_tiles = s // tq
    grid = (b, N_HEADS // heads, n_tiles)
    return pl.pallas_call(
        functools.partial(_attn_kernel, tq=tq, tk=tk, heads=heads),
        grid=grid,
        in_specs=[pl.BlockSpec((1, heads * HEAD_SLAB, tq), lambda bi, h, i: (bi, h, i)),
                  pl.BlockSpec((1, heads * HEAD_SLAB, tq),
                               lambda bi, h, i: (bi, h, jnp.minimum(i + 1, n_tiles - 1))),
                  pl.BlockSpec((1, s, heads * HEAD_SLAB), lambda bi, h, i: (bi, 0, h)),
                  pl.BlockSpec((1, heads, nkb, V_HEAD, tk), lambda bi, h, i: (bi, h, 0, 0, 0),
                               pipeline_mode=pl.Buffered(1))],
        out_specs=pl.BlockSpec((1, tq, heads * V_HEAD), lambda bi, h, i: (bi, i, h)),
        out_shape=jax.ShapeDtypeStruct((b, s, D_ATTN), BF16),
        scratch_shapes=[pltpu.VMEM((heads, tk, tq), F32), pltpu.VMEM((heads, tk, tq), F32),
                        pltpu.VMEM((heads, 1, tq), F32), pltpu.VMEM((heads, 1, tq), F32),
                        pltpu.VMEM((heads, V_ROWS, tq), F32), pltpu.VMEM((heads, 1, tq), F32)],
        compiler_params=pltpu.CompilerParams(dimension_semantics=("arbitrary", "arbitrary", "arbitrary"),
                                             vmem_limit_bytes=ATTN_VMEM_LIMIT_BYTES),
        name="attn_prompt",
    )(qT, qT, k, vT)


def _sattn_kernel(q_ref, latp_ref, krp_ref, latn_ref, krn_ref, w_ukT_ref, w_uv_ref, o_ref, o_scr):
    nb, sd, _ = q_ref.shape
    streams = range(nb)
    lats, krs, qs = [], [], []
    for b in streams:
        lats.append((latp_ref[b].astype(BF16), latn_ref[b].astype(BF16)))
        krs.append((krp_ref[b].astype(BF16), krn_ref[b].astype(BF16)))
        qa, qr = [], []
        for h in range(N_HEADS):
            qh = q_ref[b, :, h * HEAD_SLAB:(h + 1) * HEAD_SLAB]
            qa.append(_dot(qh, w_ukT_ref[h]).astype(BF16))
            qr.append(qh[:, 0:QK_ROPE])
        qs.append((jnp.concatenate(qa, axis=0), jnp.concatenate(qr, axis=0)))
    scores = []
    for (latp, latn), (krp_t, krn), (qa, qr) in zip(lats, krs, qs):
        s_p = _dot_nt(qa, latp) + _dot(qr, krp_t)
        s_n = _dot_nt(qa, latn) + _dot_nt(qr, krn)
        scores.append((s_p, s_n))
    o_lats = []
    for (latp, latn), (s_p, s_n) in zip(lats, scores):
        m = jnp.maximum(jnp.max(s_p, axis=-1, keepdims=True), jnp.max(s_n, axis=-1, keepdims=True))
        p_p = jnp.exp(s_p - m)
        p_n = jnp.exp(s_n - m)
        l = jnp.sum(p_p, axis=-1, keepdims=True) + jnp.sum(p_n, axis=-1, keepdims=True)
        o_lats.append((_dot(p_p.astype(BF16), latp) + _dot(p_n.astype(BF16), latn)) / l)
    for b, o_lat in zip(streams, o_lats):
        o_all = _dot(o_lat.astype(BF16), w_uv_ref[...])
        for h in range(N_HEADS):
            o_scr[b, :, h * V_HEAD:(h + 1) * V_HEAD] = o_all[h * sd:(h + 1) * sd, h * V_HEAD:(h + 1) * V_HEAD]
    o_ref[...] = o_scr[...].astype(BF16)


def _sattn_call(q, lat_past, kr_past_t, lat_new, kr_new, w_ukT, w_uv, *, nb):
    b, sd, _ = q.shape
    p = lat_past.shape[1]
    assert (p + sd - 1) // CHUNK <= p // CHUNK and b % nb == 0
    return pl.pallas_call(
        _sattn_kernel,
        grid=(b // nb,),
        in_specs=[pl.BlockSpec((nb, sd, D_SLAB), lambda i: (i, 0, 0)),
                  pl.BlockSpec((nb, p, KV_LORA), lambda i: (i, 0, 0)),
                  pl.BlockSpec((nb, QK_ROPE, p), lambda i: (i, 0, 0)),
                  pl.BlockSpec((nb, sd, KV_LORA), lambda i: (i, 0, 0)),
                  pl.BlockSpec((nb, sd, QK_ROPE), lambda i: (i, 0, 0)),
                  _const_spec((N_HEADS, HEAD_SLAB, KV_LORA)), _const_spec((KV_LORA, D_ATTN))],
        out_specs=pl.BlockSpec((nb, sd, D_ATTN), lambda i: (i, 0, 0)),
        out_shape=jax.ShapeDtypeStruct((b, sd, D_ATTN), BF16),
        scratch_shapes=[pltpu.VMEM((nb, sd, D_ATTN), F32)],
        compiler_params=pltpu.CompilerParams(dimension_semantics=("arbitrary",),
                                             vmem_limit_bytes=VMEM_LIMIT_BYTES),
        name="attn_sample",
    )(q, lat_past, kr_past_t, lat_new, kr_new, w_ukT, w_uv)


def _post_kernel(x_ref, attn_ref, yconv_ref, mod_ref, w_oa_ref, w_oc_ref, gffn_ref,
                 w_gate_ref, w_up_ref, w_down_ref, gfin_ref, y_ref, *, n_chunks, row0):
    nb, ts, d = x_ref.shape
    nr = ts // n_chunks
    rows = nb * nr
    _, _, gate1, shift2, scale2, gate2 = _mod_vectors(mod_ref, row0, nb)
    spans = [slice(c * nr, (c + 1) * nr) for c in range(n_chunks)]

    mixes = []
    for sp in spans:
        a = attn_ref[:, sp, :].reshape(rows, D_ATTN)
        yc = yconv_ref[:, sp, :].reshape(rows, D_CONV)
        mixes.append(_dot(a, w_oa_ref[...]) + _dot(yc, w_oc_ref[...]))
    x1s, gus = [], []
    for sp, mix in zip(spans, mixes):
        x1 = x_ref[:, sp, :] + gate1 * mix.reshape(nb, nr, d)
        h = _rms(x1, gffn_ref[...]) * (1.0 + scale2) + shift2
        hb = h.reshape(rows, d).astype(BF16)
        x1s.append(x1)
        gus.append((_dot(hb, w_gate_ref[...]), _dot(hb, w_up_ref[...])))
    ffs = []
    for gate, up in gus:
        act = ((gate * jax.nn.sigmoid(gate)) * up).astype(BF16)
        ffs.append(_dot(act, w_down_ref[...]))
    for sp, x1, ff in zip(spans, x1s, ffs):
        x2 = x1 + gate2 * ff.reshape(nb, nr, d)
        y_ref[:, sp, :] = _rms(x2, gfin_ref[...])


def _post_call(x, attn, yconv, mod, row0, w, *, nb, ts, n_chunks=1):
    b, s, d = x.shape
    dff = w["w_gate"].shape[1]
    assert ts % n_chunks == 0 and (ts // n_chunks) % 16 == 0
    tok = lambda width: pl.BlockSpec((nb, ts, width), lambda i, j: (i, j, 0))
    return pl.pallas_call(
        functools.partial(_post_kernel, n_chunks=n_chunks, row0=row0),
        grid=(b // nb, s // ts),
        in_specs=[tok(d), tok(D_ATTN), tok(D_CONV),
                  _const_spec(mod.shape),
                  _const_spec((D_ATTN, d)), _const_spec((D_CONV, d)), _const_spec((1, d)),
                  _const_spec((d, dff)), _const_spec((d, dff)), _const_spec((dff, d)),
                  _const_spec((1, d))],
        out_specs=tok(d),
        out_shape=jax.ShapeDtypeStruct((b, s, d), F32),
        compiler_params=pltpu.CompilerParams(dimension_semantics=("arbitrary", "arbitrary"),
                                             vmem_limit_bytes=VMEM_LIMIT_BYTES),
        name="post",
    )(x, attn, yconv, mod, w["w_oa"], w["w_oc"], w["g_ffn"], w["w_gate"], w["w_up"], w["w_down"], w["g_fin"])


def _prep_weights(w_in, w_uq, w_ukv, w_out, norm_mix_g, q_norm_g, kv_norm_g, conv_w, conv_b,
                  norm_ffn_g, w_gate, w_up, w_down, final_norm_g):
    d = w_in.shape[0]
    i1 = Q_LORA
    i2 = i1 + KV_LORA
    i3 = i2 + QK_ROPE
    w_in_t = w_in.astype(BF16).T
    zpad = jnp.zeros((HEAD_SLAB - QK_ROPE, d), BF16)
    k_r = w_in_t[i2:i3]
    w_in_p = jnp.concatenate([w_in_t[:i2], k_r, zpad, k_r[HALF:], k_r[:HALF], zpad, w_in_t[i3:]], axis=0).T
    assert w_in_p.shape[1] == D_PROJ

    wqT = w_uq.astype(BF16).T.reshape(N_HEADS, QK_NOPE + QK_ROPE, Q_LORA)
    zq = jnp.zeros((N_HEADS, HEAD_SLAB - QK_NOPE - QK_ROPE, Q_LORA), BF16)
    w_uqT = jnp.concatenate([wqT[:, QK_NOPE:], wqT[:, :QK_NOPE], zq], axis=1).reshape(D_SLAB, Q_LORA)
    zq2 = jnp.zeros((N_HEADS, HEAD_SLAB - QK_ROPE, Q_LORA), BF16)
    w_uqsT = jnp.concatenate([wqT[:, QK_NOPE + HALF:], wqT[:, QK_NOPE:QK_NOPE + HALF], zq2],
                             axis=1).reshape(D_SLAB, Q_LORA)
    wkvT = w_ukv.astype(BF16).T.reshape(N_HEADS, QK_NOPE + V_HEAD, KV_LORA)
    zk1 = jnp.zeros((N_HEADS, QK_ROPE, KV_LORA), BF16)
    zk2 = jnp.zeros((N_HEADS, HEAD_SLAB - QK_ROPE - QK_NOPE, KV_LORA), BF16)
    w_ukT = jnp.concatenate([zk1, wkvT[:, :QK_NOPE], zk2], axis=1)
    w_uvT = wkvT[:, QK_NOPE:].reshape(D_ATTN, KV_LORA)

    return {
        "g_mix": norm_mix_g.reshape(1, -1), "w_in": w_in_p,
        "q_g": q_norm_g.reshape(1, -1), "kv_g": kv_norm_g.reshape(1, -1),
        "conv_w": conv_w, "conv_b": conv_b.reshape(1, -1),
        "w_uq": w_uqT.T, "w_uqs": w_uqsT.T,
        "w_uqT": jnp.concatenate([wqT[:, QK_NOPE:], wqT[:, :QK_NOPE]], axis=1).reshape(-1, Q_LORA),
        "w_uk": w_ukT.reshape(D_SLAB, KV_LORA).T, "w_ukT": w_ukT,
        "w_uv": w_uvT.T, "w_uvT": w_uvT,
        "w_oa": w_out[:D_ATTN].astype(BF16), "w_oc": w_out[D_ATTN:].astype(BF16),
        "g_ffn": norm_ffn_g.reshape(1, -1),
        "w_gate": w_gate.astype(BF16), "w_up": w_up.astype(BF16), "w_down": w_down.astype(BF16),
        "g_fin": final_norm_g.reshape(1, -1),
    }


def _rope_tables(pos):
    n = pos.shape[0]
    inv_freq = ROPE_THETA ** (-jnp.arange(HALF, dtype=F32) / HALF)
    ang_t = inv_freq[:, None] * pos.astype(F32)[None, :]
    cos_t = jnp.cos(ang_t)
    sin_t = jnp.sin(ang_t)
    z96 = jnp.zeros((HEAD_SLAB - QK_ROPE, n), F32)
    cos128 = jnp.concatenate([cos_t, cos_t, z96], axis=0).T
    sin128 = jnp.concatenate([-sin_t, sin_t, z96], axis=0).T
    return {"cos128": cos128, "sin128": sin128, "cosT": cos_t, "sinT": sin_t}


def _full_width_tables(tabs):
    n = tabs["cos128"].shape[0]
    lane = jnp.arange(HEAD_SLAB)[None, :]
    cosf = jnp.where(lane < QK_ROPE, tabs["cos128"], jnp.ones((n, HEAD_SLAB), F32))
    return {**tabs, "cosf": jnp.tile(cosf, (1, N_HEADS)), "sinf": jnp.tile(tabs["sin128"], (1, N_HEADS))}


def kernel(x_prompt, x_sample, c_prompt, c_sample, cache_kv_latent, cache_k_rope, state_conv,
           w_ada, b_ada, norm_mix_g, w_in, q_norm_g, w_uq, kv_norm_g, w_ukv,
           conv_w, conv_b, w_out, norm_ffn_g, w_gate, w_up, w_down, final_norm_g):
    depth = w_in.shape[0]
    assert depth == 1
    bp, s, d = x_prompt.shape
    bs, sd, _ = x_sample.shape
    p = cache_kv_latent.shape[2]
    l = 0
    w = _prep_weights(w_in[l], w_uq[l], w_ukv[l], w_out[l], norm_mix_g[l], q_norm_g[l], kv_norm_g[l],
                      conv_w[l], conv_b[l], norm_ffn_g[l], w_gate[l], w_up[l], w_down[l], final_norm_g)
    tabs_p = _rope_tables(jnp.arange(s, dtype=jnp.int32))
    tabs_s = _full_width_tables(_rope_tables(p + jnp.arange(sd, dtype=jnp.int32)))

    mod = _mod_call(jnp.concatenate([c_prompt, c_sample], axis=0), w_ada[l], b_ada[l])

    tq, tk = 512, 256
    conv_zero = jnp.zeros((bp, CONV_W - 1, D_CONV), x_prompt.dtype)
    yconv_p, lat_p, kr_p_t, cv_p, qT, k_all, vT = _pre_call(
        x_prompt, mod, 0, conv_zero, w, tabs_p, nb=1, ts=1024, tk=tk, prompt=True, n_chunks=4)
    attn_p = _attn_call(qT, k_all, vT, tq=tq, tk=tk)
    y_prompt = _post_call(x_prompt, attn_p, yconv_p, mod, 0, w, nb=1, ts=1024, n_chunks=4)

    nb_s = 8
    yconv_s, lat_s, kr_s, cv_s, q_s = _pre_call(
        x_sample, mod, bp, state_conv[l], w, tabs_s, nb=nb_s, ts=sd, tk=tk, prompt=False)
    attn_s = _sattn_call(q_s, cache_kv_latent[l], jnp.swapaxes(cache_k_rope[l], 1, 2), lat_s, kr_s,
                         w["w_ukT"], w["w_uv"], nb=4)
    y_sample = _post_call(x_sample, attn_s, yconv_s, mod, bp, w, nb=nb_s, ts=sd)

    kr_p = jnp.swapaxes(kr_p_t, 1, 2)
    return (y_prompt, y_sample, lat_p[None], kr_p[None], cv_p[None], lat_s[None], kr_s[None], cv_s[None])
```

```python
import functools
from typing import NamedTuple

import jax
import jax.numpy as jnp
from jax import lax
from jax.experimental import pallas as pl
from jax.experimental.pallas import tpu as pltpu

CHUNK = 64
N_HEADS = 8
QK_NOPE = 64
QK_ROPE = 32
V_HEAD = 64
KV_LORA = 256
Q_LORA = 768
D_CONV = 512
CONV_W = 3
ROPE_THETA = 10000.0
EPS = 1e-6
SM_SCALE = (QK_NOPE + QK_ROPE) ** -0.5
LOG2E = 1.4426950408889634

HALF = QK_ROPE // 2
HEAD_SLAB = 128
D_SLAB = N_HEADS * HEAD_SLAB
D_ATTN = N_HEADS * V_HEAD
V_ROWS = V_HEAD + 16

D_LAT = Q_LORA + KV_LORA

VMEM_LIMIT_BYTES = 56 * 1024 * 1024
ATTN_VMEM_LIMIT_BYTES = 58 * 1024 * 1024

class _Tiles(NamedTuple):
    dense_rows: int
    chunk_rows: int
    attn_queries: int
    attn_keys: int
    sample_streams: int
    sample_attn_streams: int


TILES = _Tiles(dense_rows=1024, chunk_rows=256, attn_queries=512, attn_keys=256,
               sample_streams=8, sample_attn_streams=4)

BF16 = jnp.bfloat16
F32 = jnp.float32
NEG = float(jnp.finfo(jnp.float32).min)


def _rms(x, g):
    ms = jnp.mean(x * x, axis=-1, keepdims=True)
    return x * lax.rsqrt(ms + EPS) * g


def _dot(a, b):
    return jnp.dot(a, b, preferred_element_type=F32)


def _dot_nt(a, b):
    return lax.dot_general(a, b, (((1,), (1,)), ((), ())), preferred_element_type=F32)


def _const_spec(shape):
    nd = len(shape)
    return pl.BlockSpec(shape, lambda *_: (0,) * nd, pipeline_mode=pl.Buffered(1))


def _mod_kernel(c_ref, w_ref, b_ref, o_ref):
    c = c_ref[...]
    a = (c * jax.nn.sigmoid(c)).astype(BF16)
    o_ref[...] = _dot(a, w_ref[...].astype(BF16)) + b_ref[...]


def _mod_call(c_all, w_ada, b_ada):
    n, d = c_all.shape
    dn = w_ada.shape[1]
    tn = dn // 4
    return pl.pallas_call(
        _mod_kernel,
        grid=(dn // tn,),
        in_specs=[pl.BlockSpec((n, d), lambda j: (0, 0)),
                  pl.BlockSpec((d, tn), lambda j: (0, j)),
                  pl.BlockSpec((1, tn), lambda j: (0, j))],
        out_specs=pl.BlockSpec((n, tn), lambda j: (0, j)),
        out_shape=jax.ShapeDtypeStruct((n, dn), F32),
        compiler_params=pltpu.CompilerParams(dimension_semantics=("arbitrary",),
                                             vmem_limit_bytes=VMEM_LIMIT_BYTES),
        name="mod",
    )(c_all, w_ada, b_ada.reshape(1, dn))


def _mod_vectors(mod_ref, row0, nb):
    d = mod_ref.shape[1] // 6
    start = row0 + pl.program_id(0) * nb
    if nb % 8 == 0 and row0 % 8 == 0:
        start = pl.multiple_of(start, 8)
    m = mod_ref[pl.ds(start, nb), :]
    return [m[:, k * d:(k + 1) * d][:, None, :] for k in range(6)]


def _pre_norm(x_ref, mod, gmix_ref, r0, nr):
    nb, _, d = x_ref.shape
    h = _rms(x_ref[:, r0:r0 + nr, :], gmix_ref[...]) * (1.0 + mod[1]) + mod[0]
    return h.reshape(nb * nr, d).astype(BF16)


def _project(h, w_lat_ref, w_rope_ref, w_conv_ref):
    return _dot(h, w_lat_ref[...]), _dot(h, w_rope_ref[...]), _dot(h, w_conv_ref[...])


def _pre_mid(proj, r0, nr, kvg_ref, qg_ref, convw_ref, convb_ref, cos128, sin128,
             yconv_ref, lat_ref, carry_ref):
    p_lat, p_rope, p_conv = proj
    nb = lat_ref.shape[0]
    latent = _rms(p_lat[:, Q_LORA:D_LAT], kvg_ref[...])
    lat_ref[:, r0:r0 + nr, :] = latent.reshape(nb, nr, KV_LORA)

    g1 = p_rope[:, :HEAD_SLAB].reshape(nb, nr, HEAD_SLAB)
    g2 = p_rope[:, HEAD_SLAB:].reshape(nb, nr, HEAD_SLAB)
    kr128 = g1 * cos128[None] + g2 * sin128[None]

    u = p_conv[:, :D_CONV]
    g_b = p_conv[:, D_CONV:2 * D_CONV].reshape(nb, nr, D_CONV)
    g_c = p_conv[:, 2 * D_CONV:]
    gated = (g_c * u).reshape(nb, nr, D_CONV)
    carry_ref[:, 8 + r0:8 + r0 + nr, :] = gated
    prev2 = carry_ref[:, 6 + r0:6 + r0 + nr, :]
    prev1 = carry_ref[:, 7 + r0:7 + r0 + nr, :]
    cw = convw_ref[...]
    conv = cw[0:1][None] * prev2 + cw[1:2][None] * prev1 + cw[2:3][None] * gated + convb_ref[...][None]
    yconv_ref[:, r0:r0 + nr, :] = (g_b * conv).astype(BF16)

    cqn = _rms(p_lat[:, :Q_LORA], qg_ref[...]).astype(BF16)
    return latent.astype(BF16), kr128, cqn, gated[:, nr - 2:nr, :]


def _pre_prompt_kernel(x_ref, mod_ref, cprev_ref, gmix_ref, w_lat_ref, w_rope_ref, w_conv_ref,
                       kvg_ref, convw_ref, convb_ref,
                       qg_ref, w_uqT_ref, w_uk_ref, w_uvT_ref, cosT_ref, sinT_ref,
                       yconv_ref, lat_ref, kropeT_ref, cstate_ref, qT_ref, k_ref, vT_ref,
                       carry_ref, *, tk, n_chunks, row0):
    _, ts, _ = x_ref.shape
    nr = ts // n_chunks
    qk = QK_NOPE + QK_ROPE

    @pl.when(pl.program_id(1) == 0)
    def _():
        carry_ref[:, 6:8, :] = cprev_ref[...]

    mod = _mod_vectors(mod_ref, row0, 1)
    hs = [_pre_norm(x_ref, mod, gmix_ref, c * nr, nr) for c in range(n_chunks)]
    projs = [_project(h, w_lat_ref, w_rope_ref, w_conv_ref) for h in hs]

    ups = []
    zrows = jnp.zeros((HEAD_SLAB - QK_ROPE, nr), F32)
    for c in range(n_chunks):
        r0 = c * nr
        cs = cosT_ref[:, r0:r0 + nr]
        sn = sinT_ref[:, r0:r0 + nr]
        cos128 = jnp.concatenate([cs, cs, zrows], axis=0).T
        sin128 = jnp.concatenate([-sn, sn, zrows], axis=0).T
        lat_bf, kr128, cqn, state = _pre_mid(
            projs[c], r0, nr, kvg_ref, qg_ref, convw_ref, convb_ref, cos128, sin128,
            yconv_ref, lat_ref, carry_ref)
        kr128 = kr128.reshape(nr, HEAD_SLAB)
        kropeT_ref[0, :, r0:r0 + nr] = kr128.T[0:QK_ROPE]
        qT = _dot_nt(w_uqT_ref[...], cqn) * (SM_SCALE * LOG2E)
        k_nope = _dot(lat_bf, w_uk_ref[...])
        vT = _dot_nt(w_uvT_ref[...], lat_bf)
        ups.append((qT, k_nope, vT, kr128))
    cstate_ref[...] = state
    carry_ref[:, 6:8, :] = state

    step = min(nr, tk)
    pad_rows = jnp.zeros((HEAD_SLAB - qk, nr), BF16)
    for c, (qT, k_nope, vT, kr128) in enumerate(ups):
        r0 = c * nr
        cs = cosT_ref[:, r0:r0 + nr]
        sn = sinT_ref[:, r0:r0 + nr]
        for h in range(N_HEADS):
            b0 = h * HEAD_SLAB
            x1 = qT[h * qk:h * qk + HALF]
            x2 = qT[h * qk + HALF:h * qk + QK_ROPE]
            qT_ref[0, b0:b0 + HALF, r0:r0 + nr] = (x1 * cs - x2 * sn).astype(BF16)
            qT_ref[0, b0 + HALF:b0 + QK_ROPE, r0:r0 + nr] = (x1 * sn + x2 * cs).astype(BF16)
            qT_ref[0, b0 + QK_ROPE:b0 + qk, r0:r0 + nr] = qT[h * qk + QK_ROPE:(h + 1) * qk].astype(BF16)
            qT_ref[0, b0 + qk:b0 + HEAD_SLAB, r0:r0 + nr] = pad_rows
            k_ref[0, r0:r0 + nr, b0:b0 + HEAD_SLAB] = (k_nope[:, b0:b0 + HEAD_SLAB] + kr128).astype(BF16)
            for off in range(0, nr, step):
                cb, lo = divmod(r0 + off, tk)
                vT_ref[0, h, cb, :, lo:lo + step] = vT[h * V_HEAD:(h + 1) * V_HEAD, off:off + step].astype(BF16)


def _pre_sample_kernel(x_ref, mod_ref, cprev_ref, gmix_ref, w_lat_ref, w_rope_ref, w_conv_ref,
                       kvg_ref, convw_ref, convb_ref,
                       cos128_ref, sin128_ref, qg_ref, w_uq_ref, w_uqs_ref, cosf_ref, sinf_ref,
                       yconv_ref, lat_ref, krope_ref, cstate_ref, q_ref, carry_ref, *, row0):
    nb, ts, _ = x_ref.shape
    carry_ref[:, 6:8, :] = cprev_ref[...]
    mod = _mod_vectors(mod_ref, row0, nb)
    proj = _project(_pre_norm(x_ref, mod, gmix_ref, 0, ts), w_lat_ref, w_rope_ref, w_conv_ref)
    _, kr128, cqn, state = _pre_mid(
        proj, 0, ts, kvg_ref, qg_ref, convw_ref, convb_ref, cos128_ref[...], sin128_ref[...],
        yconv_ref, lat_ref, carry_ref)
    krope_ref[...] = kr128[:, :, :QK_ROPE]
    cstate_ref[...] = state
    q = (_dot(cqn, w_uq_ref[...]) * SM_SCALE).reshape(nb, ts, D_SLAB)
    qs = (_dot(cqn, w_uqs_ref[...]) * SM_SCALE).reshape(nb, ts, D_SLAB)
    q_ref[...] = (q * cosf_ref[...][None] + qs * sinf_ref[...][None]).astype(BF16)


def _pre_call(x, mod, row0, cprev, w, tabs, *, nb, ts, tk, prompt, n_chunks=1):
    b, s, d = x.shape
    grid = (b // nb, s // ts)
    tok = lambda width: pl.BlockSpec((nb, ts, width), lambda i, j: (i, j, 0))
    per_b = lambda rows, width: pl.BlockSpec((nb, rows, width), lambda i, j: (i, 0, 0))
    in_specs = [tok(d), _const_spec(mod.shape), per_b(CONV_W - 1, D_CONV),
                _const_spec((1, d)), _const_spec((d, D_LAT)), _const_spec((d, 2 * HEAD_SLAB)),
                _const_spec((d, 3 * D_CONV)), _const_spec((1, KV_LORA)),
                _const_spec((CONV_W, D_CONV)), _const_spec((1, D_CONV))]
    args = [x, mod, cprev, w["g_mix"], w["w_lat"], w["w_rope"], w["w_conv"],
            w["kv_g"], w["conv_w"], w["conv_b"]]
    krope_spec = pl.BlockSpec((1, QK_ROPE, ts), lambda i, j: (i, 0, j)) if prompt else tok(QK_ROPE)
    krope_shape = (b, QK_ROPE, s) if prompt else (b, s, QK_ROPE)
    out_specs = [tok(D_CONV), tok(KV_LORA), krope_spec, per_b(CONV_W - 1, D_CONV)]
    out_shape = [jax.ShapeDtypeStruct((b, s, D_CONV), BF16),
                 jax.ShapeDtypeStruct((b, s, KV_LORA), F32),
                 jax.ShapeDtypeStruct(krope_shape, F32),
                 jax.ShapeDtypeStruct((b, CONV_W - 1, D_CONV), F32)]
    if prompt:
        nr = ts // n_chunks
        assert nb == 1 and ts % tk == 0 and ts % n_chunks == 0 and nr % HEAD_SLAB == 0
        assert tk % nr == 0 or nr % tk == 0
        kernel = functools.partial(_pre_prompt_kernel, tk=tk, n_chunks=n_chunks, row0=row0)
        in_specs += [_const_spec((1, Q_LORA)), _const_spec(w["w_uqT"].shape), _const_spec((KV_LORA, D_SLAB)),
                     _const_spec((D_ATTN, KV_LORA)),
                     pl.BlockSpec((HALF, ts), lambda i, j: (0, j)),
                     pl.BlockSpec((HALF, ts), lambda i, j: (0, j))]
        args += [w["q_g"], w["w_uqT"], w["w_uk"], w["w_uvT"], tabs["cosT"], tabs["sinT"]]
        out_specs += [pl.BlockSpec((1, D_SLAB, ts), lambda i, j: (i, 0, j)),
                      tok(D_SLAB),
                      pl.BlockSpec((1, N_HEADS, ts // tk, V_HEAD, tk), lambda i, j: (i, 0, j, 0, 0))]
        out_shape += [jax.ShapeDtypeStruct((b, D_SLAB, s), BF16),
                      jax.ShapeDtypeStruct((b, s, D_SLAB), BF16),
                      jax.ShapeDtypeStruct((b, N_HEADS, s // tk, V_HEAD, tk), BF16)]
    else:
        assert ts == s
        kernel = functools.partial(_pre_sample_kernel, row0=row0)
        in_specs += [_const_spec((ts, HEAD_SLAB)), _const_spec((ts, HEAD_SLAB)), _const_spec((1, Q_LORA)),
                     _const_spec((Q_LORA, D_SLAB)), _const_spec((Q_LORA, D_SLAB)),
                     _const_spec((ts, D_SLAB)), _const_spec((ts, D_SLAB))]
        args += [tabs["cos128"], tabs["sin128"], w["q_g"], w["w_uq"], w["w_uqs"], tabs["cosf"], tabs["sinf"]]
        out_specs += [tok(D_SLAB)]
        out_shape += [jax.ShapeDtypeStruct((b, s, D_SLAB), BF16)]
    return pl.pallas_call(
        kernel, grid=grid, in_specs=in_specs, out_specs=out_specs, out_shape=out_shape,
        scratch_shapes=[pltpu.VMEM((nb, ts + 8, D_CONV), F32)],
        compiler_params=pltpu.CompilerParams(dimension_semantics=("arbitrary", "arbitrary"),
                                             vmem_limit_bytes=VMEM_LIMIT_BYTES),
        name="pre_prompt" if prompt else "pre_sample",
    )(*args)


def _attn_kernel(qT_ref, qT_next_ref, k_ref, vT_ref, o_ref, s0_ref, s1_ref, mb0_ref, mb1_ref, acc_ref, m_ref,
                 *, tq, tk, heads):
    i = pl.program_id(2)
    m_ref[...] = jnp.full(m_ref.shape, NEG, F32)
    acc_ref[...] = jnp.zeros(acc_ref.shape, F32)
    even = (s0_ref, mb0_ref)
    odd = (s1_ref, mb1_ref)
    row = lax.broadcasted_iota(jnp.int32, (V_ROWS - V_HEAD, tk), 0)
    ones_rows = jnp.where(row == 0, 1.0, 0.0).astype(BF16)

    def score_head(j, dst, hh, masked, col0=0, q_ref=qT_ref):
        s_ref, mb_ref = dst
        k0 = pl.multiple_of(j * tk, tk)
        kb = k_ref[0, pl.ds(k0, tk), hh * HEAD_SLAB:(hh + 1) * HEAD_SLAB]
        qT = q_ref[0, hh * HEAD_SLAB:(hh + 1) * HEAD_SLAB, col0:tq]
        sT = _dot(kb, qT)
        if masked:
            kpos = k0 + lax.broadcasted_iota(jnp.int32, (tk, 1), 0)
            qpos = i * tq + col0 + lax.broadcasted_iota(jnp.int32, (1, tk), 1)
            diag = jnp.where(kpos <= (qpos | (CHUNK - 1)), sT[:, :tk], NEG)
            sT = diag if tq - col0 == tk else jnp.concatenate([diag, sT[:, tk:]], axis=1)
        s_ref[hh, :, col0:tq] = sT
        mb_ref[hh, :, col0:tq] = jnp.max(sT, axis=0, keepdims=True)

    def fold_head(j, src, hh, col0=0):
        s_ref, mb_ref = src
        m_old = m_ref[hh, :, col0:tq]
        m_new = jnp.maximum(m_old, mb_ref[hh, :, col0:tq])
        alpha = jnp.exp2(m_old - m_new)
        pT = jnp.exp2(s_ref[hh, :, col0:tq] - m_new).astype(BF16)
        pv = _dot(jnp.concatenate([vT_ref[0, hh, j], ones_rows], axis=0), pT)
        acc_ref[hh, :, col0:tq] = acc_ref[hh, :, col0:tq] * alpha + pv
        m_ref[hh, :, col0:tq] = m_new

    def stage(j, src, dst, masked, col0=0):
        for hh in range(heads):
            score_head(j + 1, dst, hh, masked, col0)
            fold_head(j, src, hh)

    jd = 2 * i

    @pl.when(i == 0)
    def _():
        for hh in range(heads):
            score_head(0, even, hh, True)

    def pair(j):
        stage(j, even, odd, False)
        stage(j + 1, odd, even, False)

    def body(jj, carry):
        pair(4 * jj)
        pair(4 * jj + 2)
        return carry

    n_pairs = jnp.maximum(i - 1, 0)
    lax.fori_loop(0, n_pairs >> 1, body, 0)

    @pl.when((n_pairs & 1) == 1)
    def _():
        pair(2 * n_pairs - 2)

    @pl.when(i > 0)
    def _():
        stage(jd - 2, even, odd, False)
        stage(jd - 1, odd, even, True)

    stage(jd, even, odd, True, tk)
    for hh in range(heads):
        score_head(0, even, hh, False, q_ref=qT_next_ref)
        fold_head(jd + 1, odd, hh, tk)

    outs = []
    for hh in range(heads):
        acc = acc_ref[hh]
        outs.append(acc[0:V_HEAD] / acc[V_HEAD:V_HEAD + 1])
    oT = jnp.concatenate(outs, axis=0)
    o_ref[0] = oT.T.astype(BF16)


def _attn_call(qT, k, vT, *, tq, tk, heads=N_HEADS):
    b, _, s = qT.shape
    nkb = s // tk
    assert tk % CHUNK == 0 and tq == 2 * tk and s % tq == 0 and N_HEADS % heads == 0
    n_tiles = s // tq
    grid = (b, N_HEADS // heads, n_tiles)
    return pl.pallas_call(
        functools.partial(_attn_kernel, tq=tq, tk=tk, heads=heads),
        grid=grid,
        in_specs=[pl.BlockSpec((1, heads * HEAD_SLAB, tq), lambda bi, h, i: (bi, h, i)),
                  pl.BlockSpec((1, heads * HEAD_SLAB, tq),
                               lambda bi, h, i: (bi, h, jnp.minimum(i + 1, n_tiles - 1))),
                  pl.BlockSpec((1, s, heads * HEAD_SLAB), lambda bi, h, i: (bi, 0, h)),
                  pl.BlockSpec((1, heads, nkb, V_HEAD, tk), lambda bi, h, i: (bi, h, 0, 0, 0),
                               pipeline_mode=pl.Buffered(1))],
        out_specs=pl.BlockSpec((1, tq, heads * V_HEAD), lambda bi, h, i: (bi, i, h)),
        out_shape=jax.ShapeDtypeStruct((b, s, D_ATTN), BF16),
        scratch_shapes=[pltpu.VMEM((heads, tk, tq), F32), pltpu.VMEM((heads, tk, tq), F32),
                        pltpu.VMEM((heads, 1, tq), F32), pltpu.VMEM((heads, 1, tq), F32),
                        pltpu.VMEM((heads, V_ROWS, tq), F32), pltpu.VMEM((heads, 1, tq), F32)],
        compiler_params=pltpu.CompilerParams(dimension_semantics=("arbitrary", "arbitrary", "arbitrary"),
                                             vmem_limit_bytes=ATTN_VMEM_LIMIT_BYTES),
        name="attn_prompt",
    )(qT, qT, k, vT)


def _sattn_kernel(q_ref, latp_ref, krp_ref, latn_ref, krn_ref, w_ukT_ref, w_uv_ref, o_ref, o_scr):
    nb, sd, _ = q_ref.shape
    streams = range(nb)
    lats, krs, qs = [], [], []
    for b in streams:
        lats.append((latp_ref[b].astype(BF16), latn_ref[b].astype(BF16)))
        krs.append((krp_ref[b].astype(BF16), krn_ref[b].astype(BF16)))
        qa, qr = [], []
        for h in range(N_HEADS):
            qh = q_ref[b, :, h * HEAD_SLAB:(h + 1) * HEAD_SLAB]
            qa.append(_dot(qh, w_ukT_ref[h]).astype(BF16))
            qr.append(qh[:, 0:QK_ROPE])
        qs.append((jnp.concatenate(qa, axis=0), jnp.concatenate(qr, axis=0)))
    scores = []
    for (latp, latn), (krp_t, krn), (qa, qr) in zip(lats, krs, qs):
        s_p = _dot_nt(qa, latp) + _dot(qr, krp_t)
        s_n = _dot_nt(qa, latn) + _dot_nt(qr, krn)
        scores.append((s_p, s_n))
    o_lats = []
    for (latp, latn), (s_p, s_n) in zip(lats, scores):
        m = jnp.maximum(jnp.max(s_p, axis=-1, keepdims=True), jnp.max(s_n, axis=-1, keepdims=True))
        p_p = jnp.exp(s_p - m)
        p_n = jnp.exp(s_n - m)
        l = jnp.sum(p_p, axis=-1, keepdims=True) + jnp.sum(p_n, axis=-1, keepdims=True)
        o_lats.append((_dot(p_p.astype(BF16), latp) + _dot(p_n.astype(BF16), latn)) / l)
    for b, o_lat in zip(streams, o_lats):
        o_all = _dot(o_lat.astype(BF16), w_uv_ref[...])
        for h in range(N_HEADS):
            o_scr[b, :, h * V_HEAD:(h + 1) * V_HEAD] = o_all[h * sd:(h + 1) * sd, h * V_HEAD:(h + 1) * V_HEAD]
    o_ref[...] = o_scr[...].astype(BF16)


def _sattn_call(q, lat_past, kr_past_t, lat_new, kr_new, w_ukT, w_uv, *, nb):
    b, sd, _ = q.shape
    p = lat_past.shape[1]
    assert (p + sd - 1) // CHUNK <= p // CHUNK and b % nb == 0
    return pl.pallas_call(
        _sattn_kernel,
        grid=(b // nb,),
        in_specs=[pl.BlockSpec((nb, sd, D_SLAB), lambda i: (i, 0, 0)),
                  pl.BlockSpec((nb, p, KV_LORA), lambda i: (i, 0, 0)),
                  pl.BlockSpec((nb, QK_ROPE, p), lambda i: (i, 0, 0)),
                  pl.BlockSpec((nb, sd, KV_LORA), lambda i: (i, 0, 0)),
                  pl.BlockSpec((nb, sd, QK_ROPE), lambda i: (i, 0, 0)),
                  _const_spec((N_HEADS, HEAD_SLAB, KV_LORA)), _const_spec((KV_LORA, D_ATTN))],
        out_specs=pl.BlockSpec((nb, sd, D_ATTN), lambda i: (i, 0, 0)),
        out_shape=jax.ShapeDtypeStruct((b, sd, D_ATTN), BF16),
        scratch_shapes=[pltpu.VMEM((nb, sd, D_ATTN), F32)],
        compiler_params=pltpu.CompilerParams(dimension_semantics=("arbitrary",),
                                             vmem_limit_bytes=VMEM_LIMIT_BYTES),
        name="attn_sample",
    )(q, lat_past, kr_past_t, lat_new, kr_new, w_ukT, w_uv)


def _post_kernel(x_ref, attn_ref, yconv_ref, mod_ref, w_oa_ref, w_oc_ref, gffn_ref,
                 w_gate_ref, w_up_ref, w_down_ref, gfin_ref, y_ref, *, n_chunks, row0):
    nb, ts, d = x_ref.shape
    nr = ts // n_chunks
    rows = nb * nr
    _, _, gate1, shift2, scale2, gate2 = _mod_vectors(mod_ref, row0, nb)
    spans = [slice(c * nr, (c + 1) * nr) for c in range(n_chunks)]

    mixes = []
    for sp in spans:
        a = attn_ref[:, sp, :].reshape(rows, D_ATTN)
        yc = yconv_ref[:, sp, :].reshape(rows, D_CONV)
        mixes.append(_dot(a, w_oa_ref[...]) + _dot(yc, w_oc_ref[...]))
    x1s, gus = [], []
    for sp, mix in zip(spans, mixes):
        x1 = x_ref[:, sp, :] + gate1 * mix.reshape(nb, nr, d)
        h = _rms(x1, gffn_ref[...]) * (1.0 + scale2) + shift2
        hb = h.reshape(rows, d).astype(BF16)
        x1s.append(x1)
        gus.append((_dot(hb, w_gate_ref[...]), _dot(hb, w_up_ref[...])))
    ffs = []
    for gate, up in gus:
        act = ((gate * jax.nn.sigmoid(gate)) * up).astype(BF16)
        ffs.append(_dot(act, w_down_ref[...]))
    for sp, x1, ff in zip(spans, x1s, ffs):
        x2 = x1 + gate2 * ff.reshape(nb, nr, d)
        y_ref[:, sp, :] = _rms(x2, gfin_ref[...])


def _post_call(x, attn, yconv, mod, row0, w, *, nb, ts, n_chunks=1):
    b, s, d = x.shape
    dff = w["w_gate"].shape[1]
    assert ts % n_chunks == 0 and (ts // n_chunks) % 16 == 0
    tok = lambda width: pl.BlockSpec((nb, ts, width), lambda i, j: (i, j, 0))
    return pl.pallas_call(
        functools.partial(_post_kernel, n_chunks=n_chunks, row0=row0),
        grid=(b // nb, s // ts),
        in_specs=[tok(d), tok(D_ATTN), tok(D_CONV),
                  _const_spec(mod.shape),
                  _const_spec((D_ATTN, d)), _const_spec((D_CONV, d)), _const_spec((1, d)),
                  _const_spec((d, dff)), _const_spec((d, dff)), _const_spec((dff, d)),
                  _const_spec((1, d))],
        out_specs=tok(d),
        out_shape=jax.ShapeDtypeStruct((b, s, d), F32),
        compiler_params=pltpu.CompilerParams(dimension_semantics=("arbitrary", "arbitrary"),
                                             vmem_limit_bytes=VMEM_LIMIT_BYTES),
        name="post",
    )(x, attn, yconv, mod, w["w_oa"], w["w_oc"], w["g_ffn"], w["w_gate"], w["w_up"], w["w_down"], w["g_fin"])


def _prep_weights(w_in, w_uq, w_ukv, w_out, norm_mix_g, q_norm_g, kv_norm_g, conv_w, conv_b,
                  norm_ffn_g, w_gate, w_up, w_down, final_norm_g):
    d = w_in.shape[0]
    i1 = Q_LORA
    i2 = i1 + KV_LORA
    i3 = i2 + QK_ROPE
    w_in = w_in.astype(BF16)
    zpad = jnp.zeros((d, HEAD_SLAB - QK_ROPE), BF16)
    k_r = w_in[:, i2:i3]
    w_rope = jnp.concatenate([k_r, zpad, k_r[:, HALF:], k_r[:, :HALF], zpad], axis=1)
    w_lat, w_conv = w_in[:, :i2], w_in[:, i3:]
    assert w_lat.shape[1] == D_LAT and w_rope.shape[1] == 2 * HEAD_SLAB and w_conv.shape[1] == 3 * D_CONV

    wqT = w_uq.astype(BF16).T.reshape(N_HEADS, QK_NOPE + QK_ROPE, Q_LORA)
    zq = jnp.zeros((N_HEADS, HEAD_SLAB - QK_NOPE - QK_ROPE, Q_LORA), BF16)
    w_uqT = jnp.concatenate([wqT[:, QK_NOPE:], wqT[:, :QK_NOPE], zq], axis=1).reshape(D_SLAB, Q_LORA)
    zq2 = jnp.zeros((N_HEADS, HEAD_SLAB - QK_ROPE, Q_LORA), BF16)
    w_uqsT = jnp.concatenate([wqT[:, QK_NOPE + HALF:], wqT[:, QK_NOPE:QK_NOPE + HALF], zq2],
                             axis=1).reshape(D_SLAB, Q_LORA)
    wkvT = w_ukv.astype(BF16).T.reshape(N_HEADS, QK_NOPE + V_HEAD, KV_LORA)
    zk1 = jnp.zeros((N_HEADS, QK_ROPE, KV_LORA), BF16)
    zk2 = jnp.zeros((N_HEADS, HEAD_SLAB - QK_ROPE - QK_NOPE, KV_LORA), BF16)
    w_ukT = jnp.concatenate([zk1, wkvT[:, :QK_NOPE], zk2], axis=1)
    w_uvT = wkvT[:, QK_NOPE:].reshape(D_ATTN, KV_LORA)

    return {
        "g_mix": norm_mix_g.reshape(1, -1), "w_lat": w_lat, "w_rope": w_rope, "w_conv": w_conv,
        "q_g": q_norm_g.reshape(1, -1), "kv_g": kv_norm_g.reshape(1, -1),
        "conv_w": conv_w, "conv_b": conv_b.reshape(1, -1),
        "w_uq": w_uqT.T, "w_uqs": w_uqsT.T,
        "w_uqT": jnp.concatenate([wqT[:, QK_NOPE:], wqT[:, :QK_NOPE]], axis=1).reshape(-1, Q_LORA),
        "w_uk": w_ukT.reshape(D_SLAB, KV_LORA).T, "w_ukT": w_ukT,
        "w_uv": w_uvT.T, "w_uvT": w_uvT,
        "w_oa": w_out[:D_ATTN].astype(BF16), "w_oc": w_out[D_ATTN:].astype(BF16),
        "g_ffn": norm_ffn_g.reshape(1, -1),
        "w_gate": w_gate.astype(BF16), "w_up": w_up.astype(BF16), "w_down": w_down.astype(BF16),
        "g_fin": final_norm_g.reshape(1, -1),
    }


def _rope_tables(pos):
    n = pos.shape[0]
    inv_freq = ROPE_THETA ** (-jnp.arange(HALF, dtype=F32) / HALF)
    ang_t = inv_freq[:, None] * pos.astype(F32)[None, :]
    cos_t = jnp.cos(ang_t)
    sin_t = jnp.sin(ang_t)
    z96 = jnp.zeros((HEAD_SLAB - QK_ROPE, n), F32)
    cos128 = jnp.concatenate([cos_t, cos_t, z96], axis=0).T
    sin128 = jnp.concatenate([-sin_t, sin_t, z96], axis=0).T
    return {"cos128": cos128, "sin128": sin128, "cosT": cos_t, "sinT": sin_t}


def _full_width_tables(tabs):
    n = tabs["cos128"].shape[0]
    lane = jnp.arange(HEAD_SLAB)[None, :]
    cosf = jnp.where(lane < QK_ROPE, tabs["cos128"], jnp.ones((n, HEAD_SLAB), F32))
    return {**tabs, "cosf": jnp.tile(cosf, (1, N_HEADS)), "sinf": jnp.tile(tabs["sin128"], (1, N_HEADS))}


def kernel(x_prompt, x_sample, c_prompt, c_sample, cache_kv_latent, cache_k_rope, state_conv,
           w_ada, b_ada, norm_mix_g, w_in, q_norm_g, w_uq, kv_norm_g, w_ukv,
           conv_w, conv_b, w_out, norm_ffn_g, w_gate, w_up, w_down, final_norm_g):
    depth = w_in.shape[0]
    assert depth == 1
    bp, s, d = x_prompt.shape
    bs, sd, _ = x_sample.shape
    p = cache_kv_latent.shape[2]
    l = 0
    w = _prep_weights(w_in[l], w_uq[l], w_ukv[l], w_out[l], norm_mix_g[l], q_norm_g[l], kv_norm_g[l],
                      conv_w[l], conv_b[l], norm_ffn_g[l], w_gate[l], w_up[l], w_down[l], final_norm_g)
    tabs_p = _rope_tables(jnp.arange(s, dtype=jnp.int32))
    tabs_s = _full_width_tables(_rope_tables(p + jnp.arange(sd, dtype=jnp.int32)))

    mod = _mod_call(jnp.concatenate([c_prompt, c_sample], axis=0), w_ada[l], b_ada[l])

    t = TILES
    conv_zero = jnp.zeros((bp, CONV_W - 1, D_CONV), x_prompt.dtype)
    yconv_p, lat_p, kr_p_t, cv_p, qT, k_all, vT = _pre_call(
        x_prompt, mod, 0, conv_zero, w, tabs_p, nb=1, ts=t.dense_rows, tk=t.attn_keys, prompt=True,
        n_chunks=t.dense_rows // t.chunk_rows)
    attn_p = _attn_call(qT, k_all, vT, tq=t.attn_queries, tk=t.attn_keys)
    y_prompt = _post_call(x_prompt, attn_p, yconv_p, mod, 0, w, nb=1, ts=t.dense_rows,
                          n_chunks=t.dense_rows // t.chunk_rows)

    yconv_s, lat_s, kr_s, cv_s, q_s = _pre_call(
        x_sample, mod, bp, state_conv[l], w, tabs_s, nb=t.sample_streams, ts=sd, tk=t.attn_keys, prompt=False)
    attn_s = _sattn_call(q_s, cache_kv_latent[l], jnp.swapaxes(cache_k_rope[l], 1, 2), lat_s, kr_s,
                         w["w_ukT"], w["w_uv"], nb=t.sample_attn_streams)
    y_sample = _post_call(x_sample, attn_s, yconv_s, mod, bp, w, nb=t.sample_streams, ts=sd)

    kr_p = jnp.swapaxes(kr_p_t, 1, 2)
    return (y_prompt, y_sample, lat_p[None], kr_p[None], cv_p[None], lat_s[None], kr_s[None], cv_s[None])
```

```python
import functools
from typing import NamedTuple

import jax
import jax.numpy as jnp
from jax import lax
from jax.experimental import pallas as pl
from jax.experimental.pallas import tpu as pltpu

CHUNK = 64
N_HEADS = 8
QK_NOPE = 64
QK_ROPE = 32
V_HEAD = 64
KV_LORA = 256
Q_LORA = 768
D_CONV = 512
CONV_W = 3
ROPE_THETA = 10000.0
EPS = 1e-6
SM_SCALE = (QK_NOPE + QK_ROPE) ** -0.5
LOG2E = 1.4426950408889634

HALF = QK_ROPE // 2
HEAD_SLAB = 128
D_SLAB = N_HEADS * HEAD_SLAB
D_ATTN = N_HEADS * V_HEAD
V_ROWS = V_HEAD + 16

D_LAT = Q_LORA + KV_LORA

VMEM_LIMIT_BYTES = 56 * 1024 * 1024
ATTN_VMEM_LIMIT_BYTES = 58 * 1024 * 1024

class _Tiles(NamedTuple):
    dense_rows: int
    chunk_rows: int
    attn_queries: int
    attn_keys: int
    sample_streams: int
    sample_attn_streams: int


TILES = _Tiles(dense_rows=1024, chunk_rows=256, attn_queries=512, attn_keys=256,
               sample_streams=8, sample_attn_streams=4)

BF16 = jnp.bfloat16
F32 = jnp.float32
NEG = float(jnp.finfo(jnp.float32).min)


def _rms(x, g):
    ms = jnp.mean(x * x, axis=-1, keepdims=True)
    return x * lax.rsqrt(ms + EPS) * g


def _dot(a, b):
    return jnp.dot(a, b, preferred_element_type=F32)


def _dot_nt(a, b):
    return lax.dot_general(a, b, (((1,), (1,)), ((), ())), preferred_element_type=F32)


def _const_spec(shape):
    nd = len(shape)
    return pl.BlockSpec(shape, lambda *_: (0,) * nd, pipeline_mode=pl.Buffered(1))


def _mod_kernel(c_ref, w_ref, b_ref, o_ref):
    c = c_ref[...]
    a = (c * jax.nn.sigmoid(c)).astype(BF16)
    o_ref[...] = _dot(a, w_ref[...].astype(BF16)) + b_ref[...]


def _mod_call(c_all, w_ada, b_ada):
    n, d = c_all.shape
    dn = w_ada.shape[1]
    tn = dn // 4
    return pl.pallas_call(
        _mod_kernel,
        grid=(dn // tn,),
        in_specs=[pl.BlockSpec((n, d), lambda j: (0, 0)),
                  pl.BlockSpec((d, tn), lambda j: (0, j)),
                  pl.BlockSpec((1, tn), lambda j: (0, j))],
        out_specs=pl.BlockSpec((n, tn), lambda j: (0, j)),
        out_shape=jax.ShapeDtypeStruct((n, dn), F32),
        compiler_params=pltpu.CompilerParams(dimension_semantics=("arbitrary",),
                                             vmem_limit_bytes=VMEM_LIMIT_BYTES),
        name="mod",
    )(c_all, w_ada, b_ada.reshape(1, dn))


def _mod_vectors(mod_ref, row0, nb):
    d = mod_ref.shape[1] // 6
    start = row0 + pl.program_id(0) * nb
    if nb % 8 == 0 and row0 % 8 == 0:
        start = pl.multiple_of(start, 8)
    m = mod_ref[pl.ds(start, nb), :]
    return [m[:, k * d:(k + 1) * d][:, None, :] for k in range(6)]


def _pre_norm(x_ref, mod, gmix_ref, r0, nr):
    nb, _, d = x_ref.shape
    h = _rms(x_ref[:, r0:r0 + nr, :], gmix_ref[...]) * (1.0 + mod[1]) + mod[0]
    return h.reshape(nb * nr, d).astype(BF16)


def _project(h, w_lat_ref, w_rope_ref, w_conv_ref):
    return _dot(h, w_lat_ref[...]), _dot(h, w_rope_ref[...]), _dot(h, w_conv_ref[...])


def _pre_mid(proj, r0, nr, kvg_ref, qg_ref, convw_ref, convb_ref, cos128, sin128,
             yconv_ref, lat_ref, carry_ref):
    p_lat, p_rope, p_conv = proj
    nb = lat_ref.shape[0]
    latent = _rms(p_lat[:, Q_LORA:D_LAT], kvg_ref[...])
    lat_ref[:, r0:r0 + nr, :] = latent.reshape(nb, nr, KV_LORA)

    g1 = p_rope[:, :HEAD_SLAB].reshape(nb, nr, HEAD_SLAB)
    g2 = p_rope[:, HEAD_SLAB:].reshape(nb, nr, HEAD_SLAB)
    kr128 = g1 * cos128[None] + g2 * sin128[None]

    u = p_conv[:, :D_CONV]
    g_b = p_conv[:, D_CONV:2 * D_CONV].reshape(nb, nr, D_CONV)
    g_c = p_conv[:, 2 * D_CONV:]
    gated = (g_c * u).reshape(nb, nr, D_CONV)
    carry_ref[:, 8 + r0:8 + r0 + nr, :] = gated
    prev2 = carry_ref[:, 6 + r0:6 + r0 + nr, :]
    prev1 = carry_ref[:, 7 + r0:7 + r0 + nr, :]
    cw = convw_ref[...]
    conv = cw[0:1][None] * prev2 + cw[1:2][None] * prev1 + cw[2:3][None] * gated + convb_ref[...][None]
    yconv_ref[:, r0:r0 + nr, :] = (g_b * conv).astype(BF16)

    cqn = _rms(p_lat[:, :Q_LORA], qg_ref[...]).astype(BF16)
    return latent.astype(BF16), kr128, cqn, gated[:, nr - 2:nr, :]


def _pre_prompt_kernel(x_ref, mod_ref, cprev_ref, gmix_ref, w_lat_ref, w_rope_ref, w_conv_ref,
                       kvg_ref, convw_ref, convb_ref,
                       qg_ref, w_uqT_ref, w_uk_ref, w_uvT_ref, cosT_ref, sinT_ref,
                       yconv_ref, lat_ref, kropeT_ref, cstate_ref, qT_ref, k_ref, vT_ref,
                       carry_ref, *, tk, n_chunks, row0):
    _, ts, _ = x_ref.shape
    nr = ts // n_chunks
    qk = QK_NOPE + QK_ROPE

    @pl.when(pl.program_id(1) == 0)
    def _():
        carry_ref[:, 6:8, :] = cprev_ref[...]

    mod = _mod_vectors(mod_ref, row0, 1)
    hs = [_pre_norm(x_ref, mod, gmix_ref, c * nr, nr) for c in range(n_chunks)]
    projs = [_project(h, w_lat_ref, w_rope_ref, w_conv_ref) for h in hs]

    ups = []
    zrows = jnp.zeros((HEAD_SLAB - QK_ROPE, nr), F32)
    for c in range(n_chunks):
        r0 = c * nr
        cs = cosT_ref[:, r0:r0 + nr]
        sn = sinT_ref[:, r0:r0 + nr]
        cos128 = jnp.concatenate([cs, cs, zrows], axis=0).T
        sin128 = jnp.concatenate([-sn, sn, zrows], axis=0).T
        lat_bf, kr128, cqn, state = _pre_mid(
            projs[c], r0, nr, kvg_ref, qg_ref, convw_ref, convb_ref, cos128, sin128,
            yconv_ref, lat_ref, carry_ref)
        kr128 = kr128.reshape(nr, HEAD_SLAB)
        kropeT_ref[0, :, r0:r0 + nr] = kr128.T[0:QK_ROPE]
        qT = _dot_nt(w_uqT_ref[...], cqn) * (SM_SCALE * LOG2E)
        k_nope = _dot(lat_bf, w_uk_ref[...])
        vT = _dot_nt(w_uvT_ref[...], lat_bf)
        ups.append((qT, k_nope, vT, kr128))
    cstate_ref[...] = state
    carry_ref[:, 6:8, :] = state

    step = min(nr, tk)
    pad_rows = jnp.zeros((HEAD_SLAB - qk, nr), BF16)
    for c, (qT, k_nope, vT, kr128) in enumerate(ups):
        r0 = c * nr
        cs = cosT_ref[:, r0:r0 + nr]
        sn = sinT_ref[:, r0:r0 + nr]
        for h in range(N_HEADS):
            b0 = h * HEAD_SLAB
            x1 = qT[h * qk:h * qk + HALF]
            x2 = qT[h * qk + HALF:h * qk + QK_ROPE]
            qT_ref[0, b0:b0 + HALF, r0:r0 + nr] = (x1 * cs - x2 * sn).astype(BF16)
            qT_ref[0, b0 + HALF:b0 + QK_ROPE, r0:r0 + nr] = (x1 * sn + x2 * cs).astype(BF16)
            qT_ref[0, b0 + QK_ROPE:b0 + qk, r0:r0 + nr] = qT[h * qk + QK_ROPE:(h + 1) * qk].astype(BF16)
            qT_ref[0, b0 + qk:b0 + HEAD_SLAB, r0:r0 + nr] = pad_rows
            k_ref[0, r0:r0 + nr, b0:b0 + HEAD_SLAB] = (k_nope[:, b0:b0 + HEAD_SLAB] + kr128).astype(BF16)
            for off in range(0, nr, step):
                cb, lo = divmod(r0 + off, tk)
                vT_ref[0, h, cb, :, lo:lo + step] = vT[h * V_HEAD:(h + 1) * V_HEAD, off:off + step].astype(BF16)


def _pre_sample_kernel(x_ref, mod_ref, cprev_ref, gmix_ref, w_lat_ref, w_rope_ref, w_conv_ref,
                       kvg_ref, convw_ref, convb_ref,
                       cos128_ref, sin128_ref, qg_ref, w_uq_ref, w_uqs_ref, cosf_ref, sinf_ref,
                       yconv_ref, lat_ref, krope_ref, cstate_ref, q_ref, carry_ref, *, row0):
    nb, ts, _ = x_ref.shape
    carry_ref[:, 6:8, :] = cprev_ref[...]
    mod = _mod_vectors(mod_ref, row0, nb)
    proj = _project(_pre_norm(x_ref, mod, gmix_ref, 0, ts), w_lat_ref, w_rope_ref, w_conv_ref)
    _, kr128, cqn, state = _pre_mid(
        proj, 0, ts, kvg_ref, qg_ref, convw_ref, convb_ref, cos128_ref[...], sin128_ref[...],
        yconv_ref, lat_ref, carry_ref)
    krope_ref[...] = kr128[:, :, :QK_ROPE]
    cstate_ref[...] = state
    q = (_dot(cqn, w_uq_ref[...]) * SM_SCALE).reshape(nb, ts, D_SLAB)
    qs = (_dot(cqn, w_uqs_ref[...]) * SM_SCALE).reshape(nb, ts, D_SLAB)
    q_ref[...] = (q * cosf_ref[...][None] + qs * sinf_ref[...][None]).astype(BF16)


def _pre_call(x, mod, row0, cprev, w, tabs, *, nb, ts, tk, prompt, n_chunks=1):
    b, s, d = x.shape
    grid = (b // nb, s // ts)
    tok = lambda width: pl.BlockSpec((nb, ts, width), lambda i, j: (i, j, 0))
    per_b = lambda rows, width: pl.BlockSpec((nb, rows, width), lambda i, j: (i, 0, 0))
    in_specs = [tok(d), _const_spec(mod.shape), per_b(CONV_W - 1, D_CONV),
                _const_spec((1, d)), _const_spec((d, D_LAT)), _const_spec((d, 2 * HEAD_SLAB)),
                _const_spec((d, 3 * D_CONV)), _const_spec((1, KV_LORA)),
                _const_spec((CONV_W, D_CONV)), _const_spec((1, D_CONV))]
    args = [x, mod, cprev, w["g_mix"], w["w_lat"], w["w_rope"], w["w_conv"],
            w["kv_g"], w["conv_w"], w["conv_b"]]
    krope_spec = pl.BlockSpec((1, QK_ROPE, ts), lambda i, j: (i, 0, j)) if prompt else tok(QK_ROPE)
    krope_shape = (b, QK_ROPE, s) if prompt else (b, s, QK_ROPE)
    out_specs = [tok(D_CONV), tok(KV_LORA), krope_spec, per_b(CONV_W - 1, D_CONV)]
    out_shape = [jax.ShapeDtypeStruct((b, s, D_CONV), BF16),
                 jax.ShapeDtypeStruct((b, s, KV_LORA), F32),
                 jax.ShapeDtypeStruct(krope_shape, F32),
                 jax.ShapeDtypeStruct((b, CONV_W - 1, D_CONV), F32)]
    if prompt:
        nr = ts // n_chunks
        assert nb == 1 and ts % tk == 0 and ts % n_chunks == 0 and nr % HEAD_SLAB == 0
        assert tk % nr == 0 or nr % tk == 0
        kernel = functools.partial(_pre_prompt_kernel, tk=tk, n_chunks=n_chunks, row0=row0)
        in_specs += [_const_spec((1, Q_LORA)), _const_spec(w["w_uqT"].shape), _const_spec((KV_LORA, D_SLAB)),
                     _const_spec((D_ATTN, KV_LORA)),
                     pl.BlockSpec((HALF, ts), lambda i, j: (0, j)),
                     pl.BlockSpec((HALF, ts), lambda i, j: (0, j))]
        args += [w["q_g"], w["w_uqT"], w["w_uk"], w["w_uvT"], tabs["cosT"], tabs["sinT"]]
        out_specs += [pl.BlockSpec((1, D_SLAB, ts), lambda i, j: (i, 0, j)),
                      tok(D_SLAB),
                      pl.BlockSpec((1, N_HEADS, ts // tk, V_HEAD, tk), lambda i, j: (i, 0, j, 0, 0))]
        out_shape += [jax.ShapeDtypeStruct((b, D_SLAB, s), BF16),
                      jax.ShapeDtypeStruct((b, s, D_SLAB), BF16),
                      jax.ShapeDtypeStruct((b, N_HEADS, s // tk, V_HEAD, tk), BF16)]
    else:
        assert ts == s
        kernel = functools.partial(_pre_sample_kernel, row0=row0)
        in_specs += [_const_spec((ts, HEAD_SLAB)), _const_spec((ts, HEAD_SLAB)), _const_spec((1, Q_LORA)),
                     _const_spec((Q_LORA, D_SLAB)), _const_spec((Q_LORA, D_SLAB)),
                     _const_spec((ts, D_SLAB)), _const_spec((ts, D_SLAB))]
        args += [tabs["cos128"], tabs["sin128"], w["q_g"], w["w_uq"], w["w_uqs"], tabs["cosf"], tabs["sinf"]]
        out_specs += [tok(D_SLAB)]
        out_shape += [jax.ShapeDtypeStruct((b, s, D_SLAB), BF16)]
    return pl.pallas_call(
        kernel, grid=grid, in_specs=in_specs, out_specs=out_specs, out_shape=out_shape,
        scratch_shapes=[pltpu.VMEM((nb, ts + 8, D_CONV), F32)],
        compiler_params=pltpu.CompilerParams(dimension_semantics=("arbitrary", "arbitrary"),
                                             vmem_limit_bytes=VMEM_LIMIT_BYTES),
        name="pre_prompt" if prompt else "pre_sample",
    )(*args)


def _attn_kernel(qT_ref, qT_next_ref, k_ref, vT_ref, o_ref, s0_ref, s1_ref, mb0_ref, mb1_ref, acc_ref, m_ref,
                 *, tq, tk, heads):
    i = pl.program_id(2)
    m_ref[...] = jnp.full(m_ref.shape, NEG, F32)
    acc_ref[...] = jnp.zeros(acc_ref.shape, F32)
    even = (s0_ref, mb0_ref)
    odd = (s1_ref, mb1_ref)
    row = lax.broadcasted_iota(jnp.int32, (V_ROWS - V_HEAD, tk), 0)
    ones_rows = jnp.where(row == 0, 1.0, 0.0).astype(BF16)

    def score_head(j, dst, hh, masked, col0=0, q_ref=qT_ref):
        s_ref, mb_ref = dst
        k0 = pl.multiple_of(j * tk, tk)
        kb = k_ref[0, pl.ds(k0, tk), hh * HEAD_SLAB:(hh + 1) * HEAD_SLAB]
        qT = q_ref[0, hh * HEAD_SLAB:(hh + 1) * HEAD_SLAB, col0:tq]
        sT = _dot(kb, qT)
        if masked:
            kpos = k0 + lax.broadcasted_iota(jnp.int32, (tk, 1), 0)
            qpos = i * tq + col0 + lax.broadcasted_iota(jnp.int32, (1, tk), 1)
            diag = jnp.where(kpos <= (qpos | (CHUNK - 1)), sT[:, :tk], NEG)
            sT = diag if tq - col0 == tk else jnp.concatenate([diag, sT[:, tk:]], axis=1)
        s_ref[hh, :, col0:tq] = sT
        mb_ref[hh, :, col0:tq] = jnp.max(sT, axis=0, keepdims=True)

    def fold_head(j, src, hh, col0=0):
        s_ref, mb_ref = src
        m_old = m_ref[hh, :, col0:tq]
        m_new = jnp.maximum(m_old, mb_ref[hh, :, col0:tq])
        alpha = jnp.exp2(m_old - m_new)
        pT = jnp.exp2(s_ref[hh, :, col0:tq] - m_new).astype(BF16)
        pv = _dot(jnp.concatenate([vT_ref[0, hh, j], ones_rows], axis=0), pT)
        acc_ref[hh, :, col0:tq] = acc_ref[hh, :, col0:tq] * alpha + pv
        m_ref[hh, :, col0:tq] = m_new

    def stage(j, src, dst, masked, col0=0):
        for hh in range(heads):
            score_head(j + 1, dst, hh, masked, col0)
            fold_head(j, src, hh)

    jd = 2 * i

    @pl.when(i == 0)
    def _():
        for hh in range(heads):
            score_head(0, even, hh, True)

    def pair(j):
        stage(j, even, odd, False)
        stage(j + 1, odd, even, False)

    def body(jj, carry):
        pair(4 * jj)
        pair(4 * jj + 2)
        return carry

    n_pairs = jnp.maximum(i - 1, 0)
    lax.fori_loop(0, n_pairs >> 1, body, 0)

    @pl.when((n_pairs & 1) == 1)
    def _():
        pair(2 * n_pairs - 2)

    @pl.when(i > 0)
    def _():
        stage(jd - 2, even, odd, False)
        stage(jd - 1, odd, even, True)

    stage(jd, even, odd, True, tk)
    for hh in range(heads):
        score_head(0, even, hh, False, q_ref=qT_next_ref)
        fold_head(jd + 1, odd, hh, tk)

    outs = []
    for hh in range(heads):
        acc = acc_ref[hh]
        outs.append(acc[0:V_HEAD] / acc[V_HEAD:V_HEAD + 1])
    oT = jnp.concatenate(outs, axis=0)
    o_ref[0] = oT.astype(BF16)


def _attn_call(qT, k, vT, *, tq, tk, heads=N_HEADS):
    b, _, s = qT.shape
    nkb = s // tk
    assert tk % CHUNK == 0 and tq == 2 * tk and s % tq == 0 and N_HEADS % heads == 0
    n_tiles = s // tq
    grid = (b, N_HEADS // heads, n_tiles)
    return pl.pallas_call(
        functools.partial(_attn_kernel, tq=tq, tk=tk, heads=heads),
        grid=grid,
        in_specs=[pl.BlockSpec((1, heads * HEAD_SLAB, tq), lambda bi, h, i: (bi, h, i)),
                  pl.BlockSpec((1, heads * HEAD_SLAB, tq),
                               lambda bi, h, i: (bi, h, jnp.minimum(i + 1, n_tiles - 1))),
                  pl.BlockSpec((1, s, heads * HEAD_SLAB), lambda bi, h, i: (bi, 0, h)),
                  pl.BlockSpec((1, heads, nkb, V_HEAD, tk), lambda bi, h, i: (bi, h, 0, 0, 0),
                               pipeline_mode=pl.Buffered(1))],
        out_specs=pl.BlockSpec((1, heads * V_HEAD, tq), lambda bi, h, i: (bi, h, i)),
        out_shape=jax.ShapeDtypeStruct((b, D_ATTN, s), BF16),
        scratch_shapes=[pltpu.VMEM((heads, tk, tq), F32), pltpu.VMEM((heads, tk, tq), F32),
                        pltpu.VMEM((heads, 1, tq), F32), pltpu.VMEM((heads, 1, tq), F32),
                        pltpu.VMEM((heads, V_ROWS, tq), F32), pltpu.VMEM((heads, 1, tq), F32)],
        compiler_params=pltpu.CompilerParams(dimension_semantics=("arbitrary", "arbitrary", "arbitrary"),
                                             vmem_limit_bytes=ATTN_VMEM_LIMIT_BYTES),
        name="attn_prompt",
    )(qT, qT, k, vT)


def _sattn_kernel(q_ref, latp_ref, krp_ref, latn_ref, krn_ref, w_ukT_ref, w_uv_ref, o_ref, o_scr):
    nb, sd, _ = q_ref.shape
    streams = range(nb)
    lats, krs, qs = [], [], []
    for b in streams:
        lats.append((latp_ref[b].astype(BF16), latn_ref[b].astype(BF16)))
        krs.append((krp_ref[b].astype(BF16), krn_ref[b].astype(BF16)))
        qa, qr = [], []
        for h in range(N_HEADS):
            qh = q_ref[b, :, h * HEAD_SLAB:(h + 1) * HEAD_SLAB]
            qa.append(_dot(qh, w_ukT_ref[h]).astype(BF16))
            qr.append(qh[:, 0:QK_ROPE])
        qs.append((jnp.concatenate(qa, axis=0), jnp.concatenate(qr, axis=0)))
    scores = []
    for (latp, latn), (krp_t, krn), (qa, qr) in zip(lats, krs, qs):
        s_p = _dot_nt(qa, latp) + _dot(qr, krp_t)
        s_n = _dot_nt(qa, latn) + _dot_nt(qr, krn)
        scores.append((s_p, s_n))
    o_lats = []
    for (latp, latn), (s_p, s_n) in zip(lats, scores):
        m = jnp.maximum(jnp.max(s_p, axis=-1, keepdims=True), jnp.max(s_n, axis=-1, keepdims=True))
        p_p = jnp.exp(s_p - m)
        p_n = jnp.exp(s_n - m)
        l = jnp.sum(p_p, axis=-1, keepdims=True) + jnp.sum(p_n, axis=-1, keepdims=True)
        o_lats.append((_dot(p_p.astype(BF16), latp) + _dot(p_n.astype(BF16), latn)) / l)
    for b, o_lat in zip(streams, o_lats):
        o_all = _dot(o_lat.astype(BF16), w_uv_ref[...])
        for h in range(N_HEADS):
            o_scr[b, :, h * V_HEAD:(h + 1) * V_HEAD] = o_all[h * sd:(h + 1) * sd, h * V_HEAD:(h + 1) * V_HEAD]
    o_ref[...] = o_scr[...].astype(BF16)


def _sattn_call(q, lat_past, kr_past_t, lat_new, kr_new, w_ukT, w_uv, *, nb):
    b, sd, _ = q.shape
    p = lat_past.shape[1]
    assert (p + sd - 1) // CHUNK <= p // CHUNK and b % nb == 0
    return pl.pallas_call(
        _sattn_kernel,
        grid=(b // nb,),
        in_specs=[pl.BlockSpec((nb, sd, D_SLAB), lambda i: (i, 0, 0)),
                  pl.BlockSpec((nb, p, KV_LORA), lambda i: (i, 0, 0)),
                  pl.BlockSpec((nb, QK_ROPE, p), lambda i: (i, 0, 0)),
                  pl.BlockSpec((nb, sd, KV_LORA), lambda i: (i, 0, 0)),
                  pl.BlockSpec((nb, sd, QK_ROPE), lambda i: (i, 0, 0)),
                  _const_spec((N_HEADS, HEAD_SLAB, KV_LORA)), _const_spec((KV_LORA, D_ATTN))],
        out_specs=pl.BlockSpec((nb, sd, D_ATTN), lambda i: (i, 0, 0)),
        out_shape=jax.ShapeDtypeStruct((b, sd, D_ATTN), BF16),
        scratch_shapes=[pltpu.VMEM((nb, sd, D_ATTN), F32)],
        compiler_params=pltpu.CompilerParams(dimension_semantics=("arbitrary",),
                                             vmem_limit_bytes=VMEM_LIMIT_BYTES),
        name="attn_sample",
    )(q, lat_past, kr_past_t, lat_new, kr_new, w_ukT, w_uv)


def _post_kernel(x_ref, attn_ref, yconv_ref, mod_ref, w_oa_ref, w_oc_ref, gffn_ref,
                 w_gate_ref, w_up_ref, w_down_ref, gfin_ref, y_ref, *, n_chunks, row0, attn_feature_major):
    nb, ts, d = x_ref.shape
    nr = ts // n_chunks
    rows = nb * nr
    _, _, gate1, shift2, scale2, gate2 = _mod_vectors(mod_ref, row0, nb)
    spans = [slice(c * nr, (c + 1) * nr) for c in range(n_chunks)]

    mixes = []
    for sp in spans:
        yc = yconv_ref[:, sp, :].reshape(rows, D_CONV)
        if attn_feature_major:
            a_t = attn_ref[0, :, sp]
            mix_a = lax.dot_general(a_t, w_oa_ref[...], (((0,), (0,)), ((), ())), preferred_element_type=F32)
        else:
            mix_a = _dot(attn_ref[:, sp, :].reshape(rows, D_ATTN), w_oa_ref[...])
        mixes.append(mix_a + _dot(yc, w_oc_ref[...]))
    x1s, gus = [], []
    for sp, mix in zip(spans, mixes):
        x1 = x_ref[:, sp, :] + gate1 * mix.reshape(nb, nr, d)
        h = _rms(x1, gffn_ref[...]) * (1.0 + scale2) + shift2
        hb = h.reshape(rows, d).astype(BF16)
        x1s.append(x1)
        gus.append((_dot(hb, w_gate_ref[...]), _dot(hb, w_up_ref[...])))
    ffs = []
    for gate, up in gus:
        act = ((gate * jax.nn.sigmoid(gate)) * up).astype(BF16)
        ffs.append(_dot(act, w_down_ref[...]))
    for sp, x1, ff in zip(spans, x1s, ffs):
        x2 = x1 + gate2 * ff.reshape(nb, nr, d)
        y_ref[:, sp, :] = _rms(x2, gfin_ref[...])


def _post_call(x, attn, yconv, mod, row0, w, *, nb, ts, n_chunks=1, attn_feature_major=False):
    b, s, d = x.shape
    dff = w["w_gate"].shape[1]
    assert ts % n_chunks == 0 and (ts // n_chunks) % 16 == 0
    tok = lambda width: pl.BlockSpec((nb, ts, width), lambda i, j: (i, j, 0))
    if attn_feature_major:
        assert nb == 1 and attn.shape == (b, D_ATTN, s)
        attn_spec = pl.BlockSpec((1, D_ATTN, ts), lambda i, j: (i, 0, j))
    else:
        attn_spec = tok(D_ATTN)
    return pl.pallas_call(
        functools.partial(_post_kernel, n_chunks=n_chunks, row0=row0, attn_feature_major=attn_feature_major),
        grid=(b // nb, s // ts),
        in_specs=[tok(d), attn_spec, tok(D_CONV),
                  _const_spec(mod.shape),
                  _const_spec((D_ATTN, d)), _const_spec((D_CONV, d)), _const_spec((1, d)),
                  _const_spec((d, dff)), _const_spec((d, dff)), _const_spec((dff, d)),
                  _const_spec((1, d))],
        out_specs=tok(d),
        out_shape=jax.ShapeDtypeStruct((b, s, d), F32),
        compiler_params=pltpu.CompilerParams(dimension_semantics=("arbitrary", "arbitrary"),
                                             vmem_limit_bytes=VMEM_LIMIT_BYTES),
        name="post",
    )(x, attn, yconv, mod, w["w_oa"], w["w_oc"], w["g_ffn"], w["w_gate"], w["w_up"], w["w_down"], w["g_fin"])


def _prep_weights(w_in, w_uq, w_ukv, w_out, norm_mix_g, q_norm_g, kv_norm_g, conv_w, conv_b,
                  norm_ffn_g, w_gate, w_up, w_down, final_norm_g):
    d = w_in.shape[0]
    i1 = Q_LORA
    i2 = i1 + KV_LORA
    i3 = i2 + QK_ROPE
    w_in = w_in.astype(BF16)
    zpad = jnp.zeros((d, HEAD_SLAB - QK_ROPE), BF16)
    k_r = w_in[:, i2:i3]
    w_rope = jnp.concatenate([k_r, zpad, k_r[:, HALF:], k_r[:, :HALF], zpad], axis=1)
    w_lat, w_conv = w_in[:, :i2], w_in[:, i3:]
    assert w_lat.shape[1] == D_LAT and w_rope.shape[1] == 2 * HEAD_SLAB and w_conv.shape[1] == 3 * D_CONV

    wqT = w_uq.astype(BF16).T.reshape(N_HEADS, QK_NOPE + QK_ROPE, Q_LORA)
    zq = jnp.zeros((N_HEADS, HEAD_SLAB - QK_NOPE - QK_ROPE, Q_LORA), BF16)
    w_uqT = jnp.concatenate([wqT[:, QK_NOPE:], wqT[:, :QK_NOPE], zq], axis=1).reshape(D_SLAB, Q_LORA)
    zq2 = jnp.zeros((N_HEADS, HEAD_SLAB - QK_ROPE, Q_LORA), BF16)
    w_uqsT = jnp.concatenate([wqT[:, QK_NOPE + HALF:], wqT[:, QK_NOPE:QK_NOPE + HALF], zq2],
                             axis=1).reshape(D_SLAB, Q_LORA)
    wkvT = w_ukv.astype(BF16).T.reshape(N_HEADS, QK_NOPE + V_HEAD, KV_LORA)
    zk1 = jnp.zeros((N_HEADS, QK_ROPE, KV_LORA), BF16)
    zk2 = jnp.zeros((N_HEADS, HEAD_SLAB - QK_ROPE - QK_NOPE, KV_LORA), BF16)
    w_ukT = jnp.concatenate([zk1, wkvT[:, :QK_NOPE], zk2], axis=1)
    w_uvT = wkvT[:, QK_NOPE:].reshape(D_ATTN, KV_LORA)

    return {
        "g_mix": norm_mix_g.reshape(1, -1), "w_lat": w_lat, "w_rope": w_rope, "w_conv": w_conv,
        "q_g": q_norm_g.reshape(1, -1), "kv_g": kv_norm_g.reshape(1, -1),
        "conv_w": conv_w, "conv_b": conv_b.reshape(1, -1),
        "w_uq": w_uqT.T, "w_uqs": w_uqsT.T,
        "w_uqT": jnp.concatenate([wqT[:, QK_NOPE:], wqT[:, :QK_NOPE]], axis=1).reshape(-1, Q_LORA),
        "w_uk": w_ukT.reshape(D_SLAB, KV_LORA).T, "w_ukT": w_ukT,
        "w_uv": w_uvT.T, "w_uvT": w_uvT,
        "w_oa": w_out[:D_ATTN].astype(BF16), "w_oc": w_out[D_ATTN:].astype(BF16),
        "g_ffn": norm_ffn_g.reshape(1, -1),
        "w_gate": w_gate.astype(BF16), "w_up": w_up.astype(BF16), "w_down": w_down.astype(BF16),
        "g_fin": final_norm_g.reshape(1, -1),
    }


def _rope_tables(pos):
    n = pos.shape[0]
    inv_freq = ROPE_THETA ** (-jnp.arange(HALF, dtype=F32) / HALF)
    ang_t = inv_freq[:, None] * pos.astype(F32)[None, :]
    cos_t = jnp.cos(ang_t)
    sin_t = jnp.sin(ang_t)
    z96 = jnp.zeros((HEAD_SLAB - QK_ROPE, n), F32)
    cos128 = jnp.concatenate([cos_t, cos_t, z96], axis=0).T
    sin128 = jnp.concatenate([-sin_t, sin_t, z96], axis=0).T
    return {"cos128": cos128, "sin128": sin128, "cosT": cos_t, "sinT": sin_t}


def _full_width_tables(tabs):
    n = tabs["cos128"].shape[0]
    lane = jnp.arange(HEAD_SLAB)[None, :]
    cosf = jnp.where(lane < QK_ROPE, tabs["cos128"], jnp.ones((n, HEAD_SLAB), F32))
    return {**tabs, "cosf": jnp.tile(cosf, (1, N_HEADS)), "sinf": jnp.tile(tabs["sin128"], (1, N_HEADS))}


def kernel(x_prompt, x_sample, c_prompt, c_sample, cache_kv_latent, cache_k_rope, state_conv,
           w_ada, b_ada, norm_mix_g, w_in, q_norm_g, w_uq, kv_norm_g, w_ukv,
           conv_w, conv_b, w_out, norm_ffn_g, w_gate, w_up, w_down, final_norm_g):
    depth = w_in.shape[0]
    assert depth == 1
    bp, s, d = x_prompt.shape
    bs, sd, _ = x_sample.shape
    p = cache_kv_latent.shape[2]
    l = 0
    w = _prep_weights(w_in[l], w_uq[l], w_ukv[l], w_out[l], norm_mix_g[l], q_norm_g[l], kv_norm_g[l],
                      conv_w[l], conv_b[l], norm_ffn_g[l], w_gate[l], w_up[l], w_down[l], final_norm_g)
    tabs_p = _rope_tables(jnp.arange(s, dtype=jnp.int32))
    tabs_s = _full_width_tables(_rope_tables(p + jnp.arange(sd, dtype=jnp.int32)))

    mod = _mod_call(jnp.concatenate([c_prompt, c_sample], axis=0), w_ada[l], b_ada[l])

    t = TILES
    conv_zero = jnp.zeros((bp, CONV_W - 1, D_CONV), x_prompt.dtype)
    yconv_p, lat_p, kr_p_t, cv_p, qT, k_all, vT = _pre_call(
        x_prompt, mod, 0, conv_zero, w, tabs_p, nb=1, ts=t.dense_rows, tk=t.attn_keys, prompt=True,
        n_chunks=t.dense_rows // t.chunk_rows)
    attn_p = _attn_call(qT, k_all, vT, tq=t.attn_queries, tk=t.attn_keys)
    y_prompt = _post_call(x_prompt, attn_p, yconv_p, mod, 0, w, nb=1, ts=t.dense_rows,
                          n_chunks=t.dense_rows // t.chunk_rows, attn_feature_major=True)

    yconv_s, lat_s, kr_s, cv_s, q_s = _pre_call(
        x_sample, mod, bp, state_conv[l], w, tabs_s, nb=t.sample_streams, ts=sd, tk=t.attn_keys, prompt=False)
    attn_s = _sattn_call(q_s, cache_kv_latent[l], jnp.swapaxes(cache_k_rope[l], 1, 2), lat_s, kr_s,
                         w["w_ukT"], w["w_uv"], nb=t.sample_attn_streams)
    y_sample = _post_call(x_sample, attn_s, yconv_s, mod, bp, w, nb=t.sample_streams, ts=sd)

    kr_p = jnp.swapaxes(kr_p_t, 1, 2)
    return (y_prompt, y_sample, lat_p[None], kr_p[None], cv_p[None], lat_s[None], kr_s[None], cv_s[None])
```
